```python
import jax
import jax.numpy as jnp
from jax import lax
import numpy as np

D_MODEL = 1024
BATCH = 16
SEQ = 4096
DEPTH = 1

N_MEM = 256
EPS = 1e-6
D_FF = 2816
HG_HEADS = 4
HG_DK = 128
HG_DV = 128
HG_CHUNK = 64
HG_K = HG_HEADS * HG_DK
HG_V = HG_HEADS * HG_DV
HG_COLS = 2 * HG_K + 2 * HG_V
DIL_GROUPS = ((128, 1), (512, 4), (2048, 16))
DIL_HEADS = 4
DIL_DH = 128
DIL_W = DIL_HEADS * DIL_DH
DIL_COLS = len(DIL_GROUPS) * 3 * DIL_W
ALIBI_HEADS = len(DIL_GROUPS) * DIL_HEADS
MEM_HEADS = 4
MEM_DH = 128
MEM_W = MEM_HEADS * MEM_DH
N_BRANCH = 3
GATE_COLS = N_BRANCH * D_MODEL
SPLITS = (HG_K, 2 * HG_K, 2 * HG_K + HG_V, HG_COLS, HG_COLS + DIL_COLS, HG_COLS + DIL_COLS + MEM_W)
D_IN = HG_COLS + DIL_COLS + MEM_W + GATE_COLS

kernel_name = 'hybrid_hgrn2_dilated_alibi_macaron'


def rmsnorm(x, w):
    xf = x.astype(jnp.float32)
    y = xf * lax.rsqrt(jnp.mean(xf * xf, axis=-1, keepdims=True) + EPS)
    return (y * w.astype(jnp.float32)).astype(x.dtype)


def swiglu(x, w_gu, w_down):
    g, u = jnp.split(x @ w_gu, 2, axis=-1)
    return (jax.nn.silu(g) * u) @ w_down


def alibi_slopes():
    return 2.0 ** (-8.0 * jnp.arange(1, ALIBI_HEADS + 1, dtype=jnp.float32) / ALIBI_HEADS)


def hgrn2(q_raw, f_raw, i_raw, lb):
    B, S = q_raw.shape[:2]
    C = HG_CHUNK
    N = S // C
    f32 = jnp.float32
    q = jax.nn.silu(q_raw.astype(f32)).reshape(B, N, C, HG_HEADS, HG_DK)
    lbh = lb.reshape(HG_HEADS, HG_DK)
    f = lbh + (1.0 - lbh) * jax.nn.sigmoid(f_raw.astype(f32).reshape(B, N, C, HG_HEADS, HG_DK))
    k = 1.0 - f
    v = i_raw.astype(f32).reshape(B, N, C, HG_HEADS, HG_DV)
    b = jnp.cumsum(jnp.log(f), axis=2)
    ref = b[:, :, C // 2 - 1:C // 2]
    scores = jnp.einsum('bnchk,bnshk->bnhcs', q * jnp.exp(b - ref), k * jnp.exp(ref - b))
    causal = jnp.tril(jnp.ones((C, C), dtype=bool))
    scores = jnp.where(causal, scores, 0.0)
    o_intra = jnp.einsum('bnhcs,bnshv->bnchv', scores, v)
    b_last = b[:, :, -1:]
    kv = jnp.einsum('bnshk,bnshv->nbhkv', k * jnp.exp(b_last - b), v)
    decay = jnp.moveaxis(jnp.exp(b_last[:, :, 0]), 1, 0)

    def step(state, inp):
        kv_n, dec_n = inp
        return dec_n[..., None] * state + kv_n, state

    s0 = jnp.zeros((B, HG_HEADS, HG_DK, HG_DV), f32)
    _, s_prev = lax.scan(step, s0, (kv, decay))
    o_inter = jnp.einsum('bnchk,nbhkv->bnchv', q * jnp.exp(b), s_prev)
    return (o_intra + o_inter).reshape(B, S, HG_HEADS, HG_DV)


def dilated_group(q, k, v, window, dil, slopes):
    B, S, H, dh = q.shape
    nk = window // dil
    span = nk * dil
    Sp = -(-S // span) * span
    L = Sp // dil
    nb = L // nk

    def to_sub(t):
        t = jnp.pad(t, ((0, 0), (0, Sp - S), (0, 0), (0, 0)))
        return t.reshape(B, L, dil, H, dh).transpose(0, 2, 1, 3, 4).reshape(B * dil, nb, nk, H, dh)

    qs, ks, vs = to_sub(q), to_sub(k), to_sub(v)
    kb = jnp.concatenate([jnp.pad(ks, ((0, 0), (1, 0), (0, 0), (0, 0), (0, 0)))[:, :-1], ks], axis=2)
    vb = jnp.concatenate([jnp.pad(vs, ((0, 0), (1, 0), (0, 0), (0, 0), (0, 0)))[:, :-1], vs], axis=2)
    s = jnp.einsum('znqhd,znkhd->znhqk', qs, kb).astype(jnp.float32) * (dh ** -0.5)
    qi = jnp.arange(nk)[:, None]
    kj = jnp.arange(2 * nk)[None, :]
    delta = nk + qi - kj
    blk = jnp.arange(nb)[:, None, None]
    valid = (delta >= 0) & (delta <= nk) & ((blk > 0) | (kj >= nk))
    s = s - slopes[:, None, None] * (delta * dil).astype(jnp.float32)
    s = jnp.where(valid[:, None], s, -jnp.inf)
    m = jnp.max(s, axis=-1)
    p = jnp.exp(s - m[..., None])
    l = jnp.sum(p, axis=-1)
    o = jnp.einsum('znhqk,znkhd->znqhd', p.astype(vb.dtype), vb).astype(jnp.float32)
    o = o.reshape(B, dil, L, H, dh).transpose(0, 2, 1, 3, 4).reshape(B, Sp, H, dh)[:, :S]

    def stat_back(t):
        return t.transpose(0, 1, 3, 2).reshape(B, dil, L, H).transpose(0, 2, 1, 3).reshape(B, Sp, H)[:, :S]

    return o, stat_back(m), stat_back(l)


def dilated_attention(dil_cols):
    B, S = dil_cols.shape[:2]
    qkv = dil_cols.reshape(B, S, len(DIL_GROUPS), 3, DIL_HEADS, DIL_DH)
    slopes = alibi_slopes()
    outs, maxs, sums = [], [], []
    for g, (window, dil) in enumerate(DIL_GROUPS):
        o, m, l = dilated_group(qkv[:, :, g, 0], qkv[:, :, g, 1], qkv[:, :, g, 2], window, dil,
                                slopes[g * DIL_HEADS:(g + 1) * DIL_HEADS])
        outs.append(o)
        maxs.append(m)
        sums.append(l)
    m_all = jnp.stack(maxs)
    w = jnp.exp(m_all - jnp.max(m_all, axis=0))
    num = jnp.sum(w[..., None] * jnp.stack(outs), axis=0)
    den = jnp.sum(w * jnp.stack(sums), axis=0)
    return (num / den[..., None]).reshape(B, S, DIL_W).astype(dil_cols.dtype)


def memory_attention(mq, mem, mem_norm_w, w_mem_kv):
    B, S = mq.shape[:2]
    mk, mv = jnp.split(rmsnorm(mem, mem_norm_w) @ w_mem_kv, 2, axis=-1)
    mq = mq.reshape(B, S, MEM_HEADS, MEM_DH)
    mk = mk.reshape(B, N_MEM, MEM_HEADS, MEM_DH)
    mv = mv.reshape(B, N_MEM, MEM_HEADS, MEM_DH)
    s = jnp.einsum('bshd,bmhd->bhsm', mq, mk).astype(jnp.float32) * (MEM_DH ** -0.5)
    p = jax.nn.softmax(s, axis=-1).astype(mv.dtype)
    return jnp.einsum('bhsm,bmhd->bshd', p, mv).reshape(B, S, MEM_W)


def token_mixing(u, mem, w_in, b_gate, lb, hg_norm_w, mem_norm_w, w_mem_kv, w_br_hg, w_br_dil, w_br_mem, w_out):
    B, S, D = u.shape
    proj = u @ w_in
    hq, hf, hi, hog, dcols, mq, gl = jnp.split(proj, SPLITS, axis=-1)
    o_hg = rmsnorm(hgrn2(hq, hf, hi, lb), hg_norm_w)
    o_hg = o_hg * jax.nn.sigmoid(hog.astype(jnp.float32)).reshape(B, S, HG_HEADS, HG_DV)
    y_hg = o_hg.reshape(B, S, HG_V).astype(u.dtype)
    y_dil = dilated_attention(dcols)
    y_mem = memory_attention(mq, mem, mem_norm_w, w_mem_kv)
    gates = jax.nn.sigmoid((gl + b_gate).astype(jnp.float32)).astype(u.dtype).reshape(B, S, N_BRANCH, D)
    y = (gates[:, :, 0] * (y_hg @ w_br_hg)
         + gates[:, :, 1] * (y_dil @ w_br_dil)
         + gates[:, :, 2] * (y_mem @ w_br_mem))
    return y @ w_out


def setup_inputs(seed: int = 0) -> dict:
    key = jax.random.key(seed)
    ks = jax.random.split(key, 22)
    f32 = jnp.float32

    def w(k, shape, fan_in):
        return jax.random.normal(k, shape, f32) * (fan_in ** -0.5)

    def gain(k, n):
        return 1.0 + 0.05 * jax.random.normal(k, (DEPTH, n), f32)

    return {
        'x': jax.random.normal(ks[0], (BATCH, SEQ, D_MODEL), f32),
        'mem': jax.random.normal(ks[1], (BATCH, N_MEM, D_MODEL), f32),
        'ffn1_pre_w': gain(ks[2], D_MODEL),
        'ffn1_w_gu': w(ks[3], (DEPTH, D_MODEL, 2 * D_FF), D_MODEL),
        'ffn1_w_down': w(ks[4], (DEPTH, D_FF, D_MODEL), D_FF),
        'ffn1_post_w': gain(ks[5], D_MODEL),
        'mix_pre_w': gain(ks[6], D_MODEL),
        'w_in': w(ks[7], (DEPTH, D_MODEL, D_IN), D_MODEL),
        'b_gate': 0.01 * jax.random.normal(ks[8], (DEPTH, GATE_COLS), f32),
        'hg_lb_logits': 0.1 * jax.random.normal(ks[9], (DEPTH + 1, HG_K), f32),
        'hg_norm_w': gain(ks[10], HG_DV),
        'mem_norm_w': gain(ks[11], D_MODEL),
        'w_mem_kv': w(ks[12], (DEPTH, D_MODEL, 2 * MEM_W), D_MODEL),
        'w_br_hg': w(ks[13], (DEPTH, HG_V, D_MODEL), HG_V),
        'w_br_dil': w(ks[14], (DEPTH, DIL_W, D_MODEL), DIL_W),
        'w_br_mem': w(ks[15], (DEPTH, MEM_W, D_MODEL), MEM_W),
        'w_out': w(ks[16], (DEPTH, D_MODEL, D_MODEL), D_MODEL),
        'mix_post_w': gain(ks[17], D_MODEL),
        'ffn2_pre_w': gain(ks[18], D_MODEL),
        'ffn2_w_gu': w(ks[19], (DEPTH, D_MODEL, 2 * D_FF), D_MODEL),
        'ffn2_w_down': w(ks[20], (DEPTH, D_FF, D_MODEL), D_FF),
        'ffn2_post_w': gain(ks[21], D_MODEL),
    }


def reference(x, mem, ffn1_pre_w, ffn1_w_gu, ffn1_w_down, ffn1_post_w, mix_pre_w, w_in, b_gate,
              hg_lb_logits, hg_norm_w, mem_norm_w, w_mem_kv, w_br_hg, w_br_dil, w_br_mem, w_out,
              mix_post_w, ffn2_pre_w, ffn2_w_gu, ffn2_w_down, ffn2_post_w):
    lb_all = jnp.cumsum(jax.nn.softmax(hg_lb_logits.astype(jnp.float32), axis=0), axis=0)
    h = x
    for l in range(DEPTH):
        h = h + 0.5 * rmsnorm(swiglu(rmsnorm(h, ffn1_pre_w[l]), ffn1_w_gu[l], ffn1_w_down[l]), ffn1_post_w[l])
        u = rmsnorm(h, mix_pre_w[l])
        y = token_mixing(u, mem, w_in[l], b_gate[l], lb_all[l], hg_norm_w[l], mem_norm_w[l], w_mem_kv[l],
                         w_br_hg[l], w_br_dil[l], w_br_mem[l], w_out[l])
        h = h + rmsnorm(y, mix_post_w[l])
        h = h + 0.5 * rmsnorm(swiglu(rmsnorm(h, ffn2_pre_w[l]), ffn2_w_gu[l], ffn2_w_down[l]), ffn2_post_w[l])
    return h
```

```python
import functools

import jax
import jax.numpy as jnp
from jax import lax
from jax.experimental import pallas as pl
from jax.experimental.pallas import tpu as pltpu

F32 = jnp.float32
BF16 = jnp.bfloat16

EPS = 1e-6
D_MODEL = 1024
D_FF = 2816
N_MEM = 256
HEADS = 4
DH = 128
BR_W = HEADS * DH
HG_CHUNK = 64
HG_COLS = 4 * BR_W
DIL_GROUPS = ((128, 1), (512, 4), (2048, 16))
NK = 128
DIL_GCOLS = 3 * BR_W
N_BRANCH = 3
ATT_SCALE = DH ** -0.5
ALIBI_HEADS = len(DIL_GROUPS) * HEADS

VMEM_LIMIT_BYTES = 56 * 1024 * 1024

FFN_TM = 512
FFN_CK = 256
HG_TS = 512
DIL_TQ = 512
MEM_TQ = 512
MIX_TM = 512


def _dot(a, b):
    return jnp.dot(a, b, preferred_element_type=F32)


def _dot_nt(a, b):
    return lax.dot_general(a, b, (((1,), (1,)), ((), ())), preferred_element_type=F32)


def _dot_tn(a, b):
    return lax.dot_general(a, b, (((0,), (0,)), ((), ())), preferred_element_type=F32)


def _rms(x, w):
    ms = jnp.mean(x * x, axis=-1, keepdims=True)
    return x * lax.rsqrt(ms + EPS) * w


def _sigmoid(x):
    return 1.0 / (1.0 + jnp.exp(-x))


def _const_spec(shape):
    zeros = (0,) * len(shape)
    return pl.BlockSpec(shape, lambda *_: zeros, pipeline_mode=pl.Buffered(1))


def _params(*semantics):
    return pltpu.CompilerParams(dimension_semantics=semantics, vmem_limit_bytes=VMEM_LIMIT_BYTES)


def _ffn_kernel(h_ref, prew_ref, wgu_ref, wd_ref, postw_ref, o_ref, a_scr):
    n = _rms(h_ref[...], prew_ref[...]).astype(BF16)
    for c in range(D_FF // FFN_CK):
        lo = c * FFN_CK
        g = _dot(n, wgu_ref[:, lo:lo + FFN_CK])
        u = _dot(n, wgu_ref[:, D_FF + lo:D_FF + lo + FFN_CK])
        a_scr[:, lo:lo + FFN_CK] = (g * _sigmoid(g) * u).astype(BF16)
    z = _dot(a_scr[...], wd_ref[...])
    o_ref[...] = h_ref[...] + 0.5 * _rms(z, postw_ref[...])


def _ffn(h, pre_w, w_gu, w_down, post_w):
    t = h.shape[0]
    return pl.pallas_call(
        _ffn_kernel,
        grid=(t // FFN_TM,),
        in_specs=[
            pl.BlockSpec((FFN_TM, D_MODEL), lambda i: (i, 0)),
            _const_spec((1, D_MODEL)),
            _const_spec((D_MODEL, 2 * D_FF)),
            _const_spec((D_FF, D_MODEL)),
            _const_spec((1, D_MODEL)),
        ],
        out_specs=pl.BlockSpec((FFN_TM, D_MODEL), lambda i: (i, 0)),
        out_shape=jax.ShapeDtypeStruct((t, D_MODEL), F32),
        scratch_shapes=[pltpu.VMEM((FFN_TM, D_FF), BF16)],
        compiler_params=_params("parallel"),
        name="ffn",
    )(h, pre_w, w_gu, w_down, post_w)


def _hgrn_kernel(h_ref, prew_ref, w_ref, lbl_ref, nw_ref, o_ref, proj_scr, st_scr):
    @pl.when(pl.program_id(1) == 0)
    def _():
        st_scr[...] = jnp.zeros_like(st_scr)

    u = _rms(h_ref[0], prew_ref[...]).astype(BF16)
    proj_scr[...] = _dot(u, w_ref[...])

    lg = lbl_ref[...]
    e = jnp.exp(lg - jnp.max(lg, axis=0, keepdims=True))
    lb = e[0:1] / jnp.sum(e, axis=0, keepdims=True)

    c_ = HG_CHUNK
    row = lax.broadcasted_iota(jnp.int32, (c_, c_), 0)
    col = lax.broadcasted_iota(jnp.int32, (c_, c_), 1)
    causal = row >= col
    tril = causal.astype(BF16)
    nw = nw_ref[...]

    for c in range(HG_TS // c_):
        r0 = c * c_
        qr = proj_scr[r0:r0 + c_, 0:BR_W]
        fr = proj_scr[r0:r0 + c_, BR_W:2 * BR_W]
        v = proj_scr[r0:r0 + c_, 2 * BR_W:3 * BR_W].astype(BF16)
        og = proj_scr[r0:r0 + c_, 3 * BR_W:4 * BR_W]
        f = lb + (1.0 - lb) * _sigmoid(fr)
        k = 1.0 - f
        lf = jnp.log(f)
        lf0 = lf.astype(BF16)
        rem = lf - lf0.astype(F32)
        lf1 = rem.astype(BF16)
        lf2 = (rem - lf1.astype(F32)).astype(BF16)
        b = _dot(tril, lf0) + _dot(tril, lf1) + _dot(tril, lf2)
        bref = b[c_ // 2 - 1:c_ // 2]
        blast = b[c_ - 1:c_]
        sq = qr * _sigmoid(qr)
        qe = (sq * jnp.exp(b - bref)).astype(BF16)
        ke = (k * jnp.exp(bref - b)).astype(BF16)
        kd = (k * jnp.exp(blast - b)).astype(BF16)
        qb = (sq * jnp.exp(b)).astype(BF16)
        dec = jnp.exp(blast)
        for h in range(HEADS):
            sl = slice(h * DH, (h + 1) * DH)
            sc = jnp.where(causal, _dot_nt(qe[:, sl], ke[:, sl]), 0.0).astype(BF16)
            st = st_scr[h]
            o = _dot(sc, v[:, sl]) + _dot_nt(qb[:, sl], st.astype(BF16))
            st_scr[h] = st * dec[:, sl] + _dot_tn(v[:, sl], kd[:, sl])
            y = _rms(o, nw) * _sigmoid(og[:, sl])
            o_ref[0, r0:r0 + c_, sl] = y.astype(BF16)


def _hgrn(h1, pre_w, w_hg, lb_logits, norm_w):
    b, s, _ = h1.shape
    return pl.pallas_call(
        _hgrn_kernel,
        grid=(b, s // HG_TS),
        in_specs=[
            pl.BlockSpec((1, HG_TS, D_MODEL), lambda bi, i: (bi, i, 0)),
            _const_spec((1, D_MODEL)),
            _const_spec((D_MODEL, HG_COLS)),
            _const_spec(lb_logits.shape),
            _const_spec((1, DH)),
        ],
        out_specs=pl.BlockSpec((1, HG_TS, BR_W), lambda bi, i: (bi, i, 0)),
        out_shape=jax.ShapeDtypeStruct((b, s, BR_W), BF16),
        scratch_shapes=[pltpu.VMEM((HG_TS, HG_COLS), F32), pltpu.VMEM((HEADS, DH, DH), F32)],
        compiler_params=_params("parallel", "arbitrary"),
        name="hgrn",
    )(h1, pre_w, w_hg, lb_logits, norm_w)


def _dil_kernel(h_ref, prew_ref, w_ref, o_ref, lse_ref, q_scr, kv_scr, *, tq, dil, slopes):
    i = pl.program_id(2)

    @pl.when(i == 0)
    def _():
        kv_scr[0:NK, :] = jnp.zeros((NK, 2 * BR_W), BF16)

    @pl.when(i > 0)
    def _():
        kv_scr[0:NK, :] = kv_scr[tq:tq + NK, :]

    u = _rms(h_ref[0], prew_ref[...]).astype(BF16)
    qkv = _dot(u, w_ref[...])
    q_scr[...] = qkv[:, 0:BR_W].astype(BF16)
    kv_scr[NK:NK + tq, :] = qkv[:, BR_W:3 * BR_W].astype(BF16)

    qi = lax.broadcasted_iota(jnp.int32, (NK, 2 * NK), 0)
    kj = lax.broadcasted_iota(jnp.int32, (NK, 2 * NK), 1)
    delta = NK + qi - kj
    band = (delta >= 0) & (delta <= NK)
    first_band = band & (kj >= jnp.where(i > 0, 0, NK))
    dist = (delta * dil).astype(F32)

    for j in range(tq // NK):
        valid = first_band if j == 0 else band
        for h in range(HEADS):
            sl = slice(h * DH, (h + 1) * DH)
            q = q_scr[j * NK:(j + 1) * NK, sl]
            kk = kv_scr[j * NK:(j + 2) * NK, sl]
            vv = kv_scr[j * NK:(j + 2) * NK, BR_W + h * DH:BR_W + (h + 1) * DH]
            s = _dot_nt(q, kk) * ATT_SCALE - slopes[h] * dist
            s = jnp.where(valid, s, -jnp.inf)
            m = jnp.max(s, axis=-1, keepdims=True)
            p = jnp.exp(s - m)
            l = jnp.sum(p, axis=-1, keepdims=True)
            o = _dot(p.astype(BF16), vv)
            o_ref[0, j * NK:(j + 1) * NK, sl] = o / l
            lse_ref[0, j * NK:(j + 1) * NK, sl] = jnp.broadcast_to(m + jnp.log(l), (NK, DH))


def _dil_group(h1, pre_w, w_g, dil, slopes):
    b, s, _ = h1.shape
    l = s // dil
    tq = min(DIL_TQ, l)
    hv = h1.reshape(b, l, dil * D_MODEL)
    kern = functools.partial(_dil_kernel, tq=tq, dil=dil, slopes=slopes)
    o, lse = pl.pallas_call(
        kern,
        grid=(b, dil, l // tq),
        in_specs=[
            pl.BlockSpec((1, tq, D_MODEL), lambda bi, r, i: (bi, i, r)),
            _const_spec((1, D_MODEL)),
            _const_spec((D_MODEL, DIL_GCOLS)),
        ],
        out_specs=[
            pl.BlockSpec((1, tq, BR_W), lambda bi, r, i: (bi, i, r)),
            pl.BlockSpec((1, tq, BR_W), lambda bi, r, i: (bi, i, r)),
        ],
        out_shape=[
            jax.ShapeDtypeStruct((b, l, dil * BR_W), F32),
            jax.ShapeDtypeStruct((b, l, dil * BR_W), F32),
        ],
        scratch_shapes=[pltpu.VMEM((tq, BR_W), BF16), pltpu.VMEM((NK + tq, 2 * BR_W), BF16)],
        compiler_params=_params("parallel", "parallel", "arbitrary"),
        name=f"dil{dil}",
    )(hv, pre_w, w_g)
    return o.reshape(b, s, BR_W), lse.reshape(b, s, BR_W)


def _memkv_kernel(m_ref, nw_ref, w_ref, o_ref):
    n = _rms(m_ref[...], nw_ref[...]).astype(BF16)
    o_ref[...] = _dot(n, w_ref[...]).astype(BF16)


def _memkv(mem2d, norm_w, w_kv):
    t = mem2d.shape[0]
    tm = 512
    return pl.pallas_call(
        _memkv_kernel,
        grid=(t // tm,),
        in_specs=[
            pl.BlockSpec((tm, D_MODEL), lambda i: (i, 0)),
            _const_spec((1, D_MODEL)),
            _const_spec((D_MODEL, 2 * BR_W)),
        ],
        out_specs=pl.BlockSpec((tm, 2 * BR_W), lambda i: (i, 0)),
        out_shape=jax.ShapeDtypeStruct((t, 2 * BR_W), BF16),
        compiler_params=_params("parallel"),
        name="memkv",
    )(mem2d, norm_w, w_kv)


def _memattn_kernel(h_ref, prew_ref, w_ref, kv_ref, o_ref):
    u = _rms(h_ref[0], prew_ref[...]).astype(BF16)
    mq = _dot(u, w_ref[...]).astype(BF16)
    for h in range(HEADS):
        sl = slice(h * DH, (h + 1) * DH)
        s = _dot_nt(mq[:, sl], kv_ref[0, :, sl]) * ATT_SCALE
        e = jnp.exp(s - jnp.max(s, axis=-1, keepdims=True))
        p = e / jnp.sum(e, axis=-1, keepdims=True)
        o = _dot(p.astype(BF16), kv_ref[0, :, BR_W + h * DH:BR_W + (h + 1) * DH])
        o_ref[0, :, sl] = o.astype(BF16)


def _memattn(h1, pre_w, w_mq, mkv):
    b, s, _ = h1.shape
    return pl.pallas_call(
        _memattn_kernel,
        grid=(b, s // MEM_TQ),
        in_specs=[
            pl.BlockSpec((1, MEM_TQ, D_MODEL), lambda bi, i: (bi, i, 0)),
            _const_spec((1, D_MODEL)),
            _const_spec((D_MODEL, BR_W)),
            pl.BlockSpec((1, N_MEM, 2 * BR_W), lambda bi, i: (bi, 0, 0)),
        ],
        out_specs=pl.BlockSpec((1, MEM_TQ, BR_W), lambda bi, i: (bi, i, 0)),
        out_shape=jax.ShapeDtypeStruct((b, s, BR_W), BF16),
        compiler_params=_params("parallel", "parallel"),
        name="memattn",
    )(h1, pre_w, w_mq, mkv)


def _mix_kernel(h_ref, prew_ref, wg_ref, bg_ref, yhg_ref, o0_ref, l0_ref, o1_ref, l1_ref, o2_ref, l2_ref,
                ymem_ref, whg_ref, wdil_ref, wmem_ref, wout_ref, postw_ref, out_ref):
    u = _rms(h_ref[...], prew_ref[...]).astype(BF16)

    def gate(bidx):
        sl = slice(bidx * D_MODEL, (bidx + 1) * D_MODEL)
        return _sigmoid(_dot(u, wg_ref[:, sl]) + bg_ref[:, sl])

    l0, l1, l2 = l0_ref[...], l1_ref[...], l2_ref[...]
    mx = jnp.maximum(jnp.maximum(l0, l1), l2)
    w0, w1, w2 = jnp.exp(l0 - mx), jnp.exp(l1 - mx), jnp.exp(l2 - mx)
    ydil = (w0 * o0_ref[...] + w1 * o1_ref[...] + w2 * o2_ref[...]) / (w0 + w1 + w2)

    y = gate(0) * _dot(yhg_ref[...], whg_ref[...])
    y = y + gate(1) * _dot(ydil.astype(BF16), wdil_ref[...])
    y = y + gate(2) * _dot(ymem_ref[...], wmem_ref[...])
    z = _dot(y.astype(BF16), wout_ref[...])
    out_ref[...] = h_ref[...] + _rms(z, postw_ref[...])


def _mix(h1, pre_w, w_gate, b_gate, y_hg, dil_outs, y_mem, w_br_hg, w_br_dil, w_br_mem, w_out, post_w):
    t = h1.shape[0]
    tm = MIX_TM
    row_d = pl.BlockSpec((tm, D_MODEL), lambda i: (i, 0))
    row_b = pl.BlockSpec((tm, BR_W), lambda i: (i, 0))
    (o0, l0), (o1, l1), (o2, l2) = dil_outs
    return pl.pallas_call(
        _mix_kernel,
        grid=(t // tm,),
        in_specs=[
            row_d,
            _const_spec((1, D_MODEL)),
            _const_spec((D_MODEL, N_BRANCH * D_MODEL)),
            _const_spec((1, N_BRANCH * D_MODEL)),
            row_b, row_b, row_b, row_b, row_b, row_b, row_b, row_b,
            _const_spec((BR_W, D_MODEL)),
            _const_spec((BR_W, D_MODEL)),
            _const_spec((BR_W, D_MODEL)),
            _const_spec((D_MODEL, D_MODEL)),
            _const_spec((1, D_MODEL)),
        ],
        out_specs=row_d,
        out_shape=jax.ShapeDtypeStruct((t, D_MODEL), F32),
        compiler_params=_params("parallel"),
        name="mix",
    )(h1, pre_w, w_gate, b_gate, y_hg, o0, l0, o1, l1, o2, l2, y_mem,
      w_br_hg, w_br_dil, w_br_mem, w_out, post_w)


def _alibi_slopes(group):
    return tuple(2.0 ** (-8.0 * (group * HEADS + h + 1) / ALIBI_HEADS) for h in range(HEADS))


def kernel(x, mem, ffn1_pre_w, ffn1_w_gu, ffn1_w_down, ffn1_post_w, mix_pre_w, w_in, b_gate, hg_lb_logits,
           hg_norm_w, mem_norm_w, w_mem_kv, w_br_hg, w_br_dil, w_br_mem, w_out, mix_post_w, ffn2_pre_w,
           ffn2_w_gu, ffn2_w_down, ffn2_post_w):
    b, s, d = x.shape
    t = b * s
    bf = lambda a: a.astype(BF16)
    depth = ffn1_pre_w.shape[0]
    h = x.reshape(t, d)
    for l in range(depth):
        assert depth == 1
        h = _ffn(h, ffn1_pre_w[l:l + 1], bf(ffn1_w_gu[l]), bf(ffn1_w_down[l]), ffn1_post_w[l:l + 1])
        h3 = h.reshape(b, s, d)
        pre_w = mix_pre_w[l:l + 1]
        w_in_l = w_in[l]
        c0 = HG_COLS
        y_hg = _hgrn(h3, pre_w, bf(w_in_l[:, :c0]), hg_lb_logits, hg_norm_w[l:l + 1])
        dil_outs = []
        for g, (_, dil) in enumerate(DIL_GROUPS):
            w_g = bf(w_in_l[:, c0 + g * DIL_GCOLS:c0 + (g + 1) * DIL_GCOLS])
            o_g, lse_g = _dil_group(h3, pre_w, w_g, dil, _alibi_slopes(g))
            dil_outs.append((o_g.reshape(t, BR_W), lse_g.reshape(t, BR_W)))
        c1 = c0 + len(DIL_GROUPS) * DIL_GCOLS
        mkv = _memkv(mem.reshape(b * N_MEM, d), mem_norm_w[l:l + 1], bf(w_mem_kv[l]))
        y_mem = _memattn(h3, pre_w, bf(w_in_l[:, c1:c1 + BR_W]), mkv.reshape(b, N_MEM, 2 * BR_W))
        c2 = c1 + BR_W
        h = _mix(h, pre_w, bf(w_in_l[:, c2:]), b_gate[l:l + 1], y_hg.reshape(t, BR_W), dil_outs,
                 y_mem.reshape(t, BR_W), bf(w_br_hg[l]), bf(w_br_dil[l]), bf(w_br_mem[l]), bf(w_out[l]),
                 mix_post_w[l:l + 1])
        h = _ffn(h, ffn2_pre_w[l:l + 1], bf(ffn2_w_gu[l]), bf(ffn2_w_down[l]), ffn2_post_w[l:l + 1])
    return h.reshape(b, s, d)
```

```python
import functools

import jax
import jax.numpy as jnp
import numpy as np
from jax import lax
from jax.experimental import pallas as pl
from jax.experimental.pallas import tpu as pltpu

F32 = jnp.float32
BF16 = jnp.bfloat16

EPS = 1e-6
D_MODEL = 1024
D_FF = 2816
N_MEM = 256
HEADS = 4
DH = 128
LANES = 128
D_SLABS = D_MODEL // LANES
BR_W = HEADS * DH
HG_CHUNK = 64
HG_COLS = 4 * BR_W
DIL_GROUPS = ((128, 1), (512, 4), (2048, 16))
PERM_DILS = tuple(d for _, d in DIL_GROUPS if d > 1)
NK = 128
DIL_GCOLS = 3 * BR_W
N_BRANCH = 3
ATT_SCALE = DH ** -0.5
ALIBI_HEADS = len(DIL_GROUPS) * HEADS

VMEM_LIMIT_BYTES = 56 * 1024 * 1024

FFN_TM = 512
FFN_CK = 256
HG_TS = 512
HG_GROUP = 256
DIL_TQ = 512
MEM_TQ = 512
MIX_TM = 512


def _dot(a, b):
    return jnp.dot(a, b, preferred_element_type=F32)


def _dot_nt(a, b):
    return lax.dot_general(a, b, (((1,), (1,)), ((), ())), preferred_element_type=F32)


def _dot_tn(a, b):
    return lax.dot_general(a, b, (((0,), (0,)), ((), ())), preferred_element_type=F32)


def _rms(x, w):
    ms = jnp.mean(x * x, axis=-1, keepdims=True)
    return x * lax.rsqrt(ms + EPS) * w


def _sigmoid(x):
    return 0.5 * jnp.tanh(0.5 * x) + 0.5


def _const_spec(shape):
    zeros = (0,) * len(shape)
    return pl.BlockSpec(shape, lambda *_: zeros, pipeline_mode=pl.Buffered(1))


def _params(*semantics):
    return pltpu.CompilerParams(dimension_semantics=semantics, vmem_limit_bytes=VMEM_LIMIT_BYTES)


def _ffn_kernel(h_ref, prew_ref, wgu_ref, wd_ref, postw_ref, o_ref, *rest, dils):
    perm_refs, a_scr = rest[:len(dils)], rest[len(dils)]
    n = _rms(h_ref[...], prew_ref[...]).astype(BF16)
    for c in range(D_FF // FFN_CK):
        lo = c * FFN_CK
        g = _dot(n, wgu_ref[:, lo:lo + FFN_CK])
        u = _dot(n, wgu_ref[:, D_FF + lo:D_FF + lo + FFN_CK])
        a_scr[:, lo:lo + FFN_CK] = (g * _sigmoid(g) * u).astype(BF16)
    z = _dot(a_scr[...], wd_ref[...])
    out = h_ref[...] + 0.5 * _rms(z, postw_ref[...])
    o_ref[...] = out
    if dils:
        slab_scr = rest[len(dils) + 1]
        for c in range(D_SLABS):
            slab_scr[c] = out[:, c * LANES:(c + 1) * LANES]
        for d, p_ref in zip(dils, perm_refs):
            for r in range(d):
                for c in range(D_SLABS):
                    p_ref[0, r, :, c * LANES:(c + 1) * LANES] = slab_scr[c, pl.ds(r, FFN_TM // d, stride=d), :]


def _ffn(h, pre_w, w_gu, w_down, post_w, *, seq, dils=()):
    t = h.shape[0]
    tiles_per_seq = seq // FFN_TM
    out_specs = [pl.BlockSpec((FFN_TM, D_MODEL), lambda i: (i, 0))]
    out_shape = [jax.ShapeDtypeStruct((t, D_MODEL), F32)]
    scratch = [pltpu.VMEM((FFN_TM, D_FF), BF16)]
    for d in dils:
        out_specs.append(pl.BlockSpec((1, d, FFN_TM // d, D_MODEL),
                                      lambda i: (i // tiles_per_seq, 0, i % tiles_per_seq, 0)))
        out_shape.append(jax.ShapeDtypeStruct((t // seq, d, seq // d, D_MODEL), F32))
    if dils:
        scratch.append(pltpu.VMEM((D_SLABS, FFN_TM, LANES), F32))
    return pl.pallas_call(
        functools.partial(_ffn_kernel, dils=dils),
        grid=(t // FFN_TM,),
        in_specs=[
            pl.BlockSpec((FFN_TM, D_MODEL), lambda i: (i, 0)),
            _const_spec((1, D_MODEL)),
            _const_spec((D_MODEL, 2 * D_FF)),
            _const_spec((D_FF, D_MODEL)),
            _const_spec((1, D_MODEL)),
        ],
        out_specs=out_specs,
        out_shape=out_shape,
        scratch_shapes=scratch,
        compiler_params=_params("parallel"),
        name="ffn",
    )(h, pre_w, w_gu, w_down, post_w)


def _hgrn_masks():
    i = np.arange(HG_TS)[:, None]
    j = np.arange(HG_TS)[None, :]
    same = (i // HG_CHUNK) == (j // HG_CHUNK)
    mid = (i // HG_CHUNK) * HG_CHUNK + HG_CHUNK // 2 - 1
    dmat = same * ((j <= i).astype(np.float32) - (j <= mid).astype(np.float32))
    c = np.arange(16)[:, None]
    smat = ((j // HG_CHUNK) == c) & (j % HG_CHUNK < HG_CHUNK // 2)
    return jnp.asarray(dmat, BF16), jnp.asarray(smat.astype(np.float32), BF16)


def _hgrn_kernel(h_ref, prew_ref, w_ref, lbl_ref, nw_ref, dmat_ref, smat_ref, o_ref, proj_scr, oint_scr, st_scr):
    @pl.when(pl.program_id(1) == 0)
    def _():
        st_scr[...] = jnp.zeros_like(st_scr)

    u = _rms(h_ref[0], prew_ref[...]).astype(BF16)
    proj_scr[:, BR_W:2 * BR_W] = _dot(u, w_ref[:, BR_W:2 * BR_W])
    proj_scr[:, 0:BR_W] = _dot(u, w_ref[:, 0:BR_W])
    proj_scr[:, 2 * BR_W:4 * BR_W] = _dot(u, w_ref[:, 2 * BR_W:4 * BR_W])

    lg = lbl_ref[...]
    e = jnp.exp(lg - jnp.max(lg, axis=0, keepdims=True))
    lb = e[0:1] / jnp.sum(e, axis=0, keepdims=True)

    c_ = HG_CHUNK
    f = lb + (1.0 - lb) * _sigmoid(proj_scr[:, BR_W:2 * BR_W])
    k = 1.0 - f
    lf = jnp.log(f)
    lf0 = lf.astype(BF16)
    lf1 = (lf - lf0.astype(F32)).astype(BF16)
    bd = _dot(dmat_ref[...], lf0) + _dot(dmat_ref[...], lf1)
    bref = _dot(smat_ref[...], lf0) + _dot(smat_ref[...], lf1)
    qr = proj_scr[:, 0:BR_W]
    qe32 = qr * _sigmoid(qr) * jnp.exp(bd)
    ke32 = k * jnp.exp(-bd)
    qe = qe32.astype(BF16)
    ke = ke32.astype(BF16)
    v = proj_scr[:, 2 * BR_W:3 * BR_W].astype(BF16)

    n_chunks = HG_TS // c_
    qb, kd, dec = [], [], []
    for c in range(n_chunks):
        rows = slice(c * c_, (c + 1) * c_)
        tail = bd[(c + 1) * c_ - 1:(c + 1) * c_]
        brc = bref[c:c + 1]
        kd.append((ke32[rows] * jnp.exp(tail)).astype(BF16))
        qb.append((qe32[rows] * jnp.exp(brc)).astype(BF16))
        dec.append(jnp.exp(brc + tail))

    row = lax.broadcasted_iota(jnp.int32, (HG_GROUP, HG_GROUP), 0)
    col = lax.broadcasted_iota(jnp.int32, (HG_GROUP, HG_GROUP), 1)
    causal = (row >= col) & ((row // c_) == (col // c_))
    nw = nw_ref[...]

    head_sl = [slice(h * DH, (h + 1) * DH) for h in range(HEADS)]
    chunk_rows = [slice(c * c_, (c + 1) * c_) for c in range(n_chunks)]
    kv_t = [[_dot_tn(v[chunk_rows[c], sl], kd[c][:, sl]) for sl in head_sl] for c in range(n_chunks)]
    st = [st_scr[h] for h in range(HEADS)]
    st_in = []
    for c in range(n_chunks):
        st_in.append([s.astype(BF16) for s in st])
        st = [st[h] * dec[c][:, head_sl[h]] + kv_t[c][h] for h in range(HEADS)]
    for h in range(HEADS):
        st_scr[h] = st[h]
    for c in range(n_chunks):
        for h, sl in enumerate(head_sl):
            oint_scr[chunk_rows[c], sl] = _dot_nt(qb[c][:, sl], st_in[c][h])

    for h in range(HEADS):
        sl = head_sl[h]
        for g in range(HG_TS // HG_GROUP):
            rows = slice(g * HG_GROUP, (g + 1) * HG_GROUP)
            sc = jnp.where(causal, _dot_nt(qe[rows, sl], ke[rows, sl]), 0.0).astype(BF16)
            o = _dot(sc, v[rows, sl]) + oint_scr[rows, sl]
            y = _rms(o, nw) * _sigmoid(proj_scr[rows, 3 * BR_W + h * DH:3 * BR_W + (h + 1) * DH])
            o_ref[0, rows, sl] = y.astype(BF16)


def _hgrn(h1, pre_w, w_hg, lb_logits, norm_w):
    b, s, _ = h1.shape
    dmat, smat = _hgrn_masks()
    return pl.pallas_call(
        _hgrn_kernel,
        grid=(b, s // HG_TS),
        in_specs=[
            pl.BlockSpec((1, HG_TS, D_MODEL), lambda bi, i: (bi, i, 0)),
            _const_spec((1, D_MODEL)),
            _const_spec((D_MODEL, HG_COLS)),
            _const_spec(lb_logits.shape),
            _const_spec((1, DH)),
            _const_spec(dmat.shape),
            _const_spec(smat.shape),
        ],
        out_specs=pl.BlockSpec((1, HG_TS, BR_W), lambda bi, i: (bi, i, 0)),
        out_shape=jax.ShapeDtypeStruct((b, s, BR_W), BF16),
        scratch_shapes=[pltpu.VMEM((HG_TS, HG_COLS), F32), pltpu.VMEM((HG_TS, BR_W), F32),
                        pltpu.VMEM((HEADS, DH, DH), F32)],
        compiler_params=_params("parallel", "arbitrary"),
        name="hgrn",
    )(h1, pre_w, w_hg, lb_logits, norm_w, dmat, smat)


def _dil_kernel(h_ref, prew_ref, w_ref, o_ref, lse_ref, q_scr, kv_scr, *, tq, dil, slopes):
    i = pl.program_id(2)

    @pl.when(i == 0)
    def _():
        kv_scr[0:NK, :] = jnp.zeros((NK, 2 * BR_W), BF16)

    @pl.when(i > 0)
    def _():
        kv_scr[0:NK, :] = kv_scr[tq:tq + NK, :]

    u = _rms(h_ref[0, 0], prew_ref[...]).astype(BF16)
    qkv = _dot(u, w_ref[...])
    q_scr[...] = qkv[:, 0:BR_W].astype(BF16)
    kv_scr[NK:NK + tq, :] = qkv[:, BR_W:3 * BR_W].astype(BF16)

    qi = lax.broadcasted_iota(jnp.int32, (NK, 2 * NK), 0)
    kj = lax.broadcasted_iota(jnp.int32, (NK, 2 * NK), 1)
    delta = NK + qi - kj
    band = (delta >= 0) & (delta <= NK)
    first_band = band & (kj >= jnp.where(i > 0, 0, NK))
    dist = (delta * dil).astype(F32)

    for j in range(tq // NK):
        valid = first_band if j == 0 else band
        for h in range(HEADS):
            sl = slice(h * DH, (h + 1) * DH)
            q = q_scr[j * NK:(j + 1) * NK, sl]
            kk = kv_scr[j * NK:(j + 2) * NK, sl]
            vv = kv_scr[j * NK:(j + 2) * NK, BR_W + h * DH:BR_W + (h + 1) * DH]
            s = _dot_nt(q, kk) * ATT_SCALE - slopes[h] * dist
            s = jnp.where(valid, s, -jnp.inf)
            m = jnp.max(s, axis=-1, keepdims=True)
            p = jnp.exp(s - m)
            l = jnp.sum(p, axis=-1, keepdims=True)
            o = _dot(p.astype(BF16), vv)
            o_ref[0, 0, j * NK:(j + 1) * NK, sl] = o / l
            lse_ref[0, 0, j * NK:(j + 1) * NK, sl] = jnp.broadcast_to(m + jnp.log(l), (NK, DH))


def _dil_group(hp, pre_w, w_g, slopes):
    b, dil, l, _ = hp.shape
    tq = min(DIL_TQ, l)
    kern = functools.partial(_dil_kernel, tq=tq, dil=dil, slopes=slopes)
    row_spec = pl.BlockSpec((1, 1, tq, BR_W), lambda bi, r, i: (bi, r, i, 0))
    return pl.pallas_call(
        kern,
        grid=(b, dil, l // tq),
        in_specs=[
            pl.BlockSpec((1, 1, tq, D_MODEL), lambda bi, r, i: (bi, r, i, 0)),
            _const_spec((1, D_MODEL)),
            _const_spec((D_MODEL, DIL_GCOLS)),
        ],
        out_specs=[row_spec, row_spec],
        out_shape=[jax.ShapeDtypeStruct((b, dil, l, BR_W), F32)] * 2,
        scratch_shapes=[pltpu.VMEM((tq, BR_W), BF16), pltpu.VMEM((NK + tq, 2 * BR_W), BF16)],
        compiler_params=_params("parallel", "parallel", "arbitrary"),
        name=f"dil{dil}",
    )(hp, pre_w, w_g)


def _memkv_kernel(m_ref, nw_ref, w_ref, o_ref):
    n = _rms(m_ref[...], nw_ref[...]).astype(BF16)
    o_ref[...] = _dot(n, w_ref[...]).astype(BF16)


def _memkv(mem2d, norm_w, w_kv):
    t = mem2d.shape[0]
    tm = 512
    return pl.pallas_call(
        _memkv_kernel,
        grid=(t // tm,),
        in_specs=[
            pl.BlockSpec((tm, D_MODEL), lambda i: (i, 0)),
            _const_spec((1, D_MODEL)),
            _const_spec((D_MODEL, 2 * BR_W)),
        ],
        out_specs=pl.BlockSpec((tm, 2 * BR_W), lambda i: (i, 0)),
        out_shape=jax.ShapeDtypeStruct((t, 2 * BR_W), BF16),
        compiler_params=_params("parallel"),
        name="memkv",
    )(mem2d, norm_w, w_kv)


def _memattn_kernel(h_ref, prew_ref, w_ref, kv_ref, o_ref):
    u = _rms(h_ref[0], prew_ref[...]).astype(BF16)
    mq = _dot(u, w_ref[...]).astype(BF16)
    for h in range(HEADS):
        sl = slice(h * DH, (h + 1) * DH)
        s = _dot_nt(mq[:, sl], kv_ref[0, :, sl]) * ATT_SCALE
        e = jnp.exp(s - jnp.max(s, axis=-1, keepdims=True))
        p = e / jnp.sum(e, axis=-1, keepdims=True)
        o = _dot(p.astype(BF16), kv_ref[0, :, BR_W + h * DH:BR_W + (h + 1) * DH])
        o_ref[0, :, sl] = o.astype(BF16)


def _memattn(h1, pre_w, w_mq, mkv):
    b, s, _ = h1.shape
    return pl.pallas_call(
        _memattn_kernel,
        grid=(b, s // MEM_TQ),
        in_specs=[
            pl.BlockSpec((1, MEM_TQ, D_MODEL), lambda bi, i: (bi, i, 0)),
            _const_spec((1, D_MODEL)),
            _const_spec((D_MODEL, BR_W)),
            pl.BlockSpec((1, N_MEM, 2 * BR_W), lambda bi, i: (bi, 0, 0)),
        ],
        out_specs=pl.BlockSpec((1, MEM_TQ, BR_W), lambda bi, i: (bi, i, 0)),
        out_shape=jax.ShapeDtypeStruct((b, s, BR_W), BF16),
        compiler_params=_params("parallel", "parallel"),
        name="memattn",
    )(h1, pre_w, w_mq, mkv)


def _mix_kernel(h_ref, prew_ref, wg_ref, bg_ref, yhg_ref, o0_ref, l0_ref, o1_ref, l1_ref, o2_ref, l2_ref,
                ymem_ref, whg_ref, wdil_ref, wmem_ref, wout_ref, postw_ref, out_ref, *slab_scrs):
    u = _rms(h_ref[...], prew_ref[...]).astype(BF16)

    def gate(bidx):
        sl = slice(bidx * D_MODEL, (bidx + 1) * D_MODEL)
        return _sigmoid(_dot(u, wg_ref[:, sl]) + bg_ref[:, sl])

    for src_ref, scr, d in zip((o1_ref, l1_ref, o2_ref, l2_ref), slab_scrs, (PERM_DILS[0],) * 2 + (PERM_DILS[1],) * 2):
        for r in range(d):
            for h in range(HEADS):
                scr[h, pl.ds(r, MIX_TM // d, stride=d), :] = src_ref[0, r, :, h * DH:(h + 1) * DH]
    o1_scr, l1_scr, o2_scr, l2_scr = slab_scrs

    heads = []
    for h in range(HEADS):
        sl = slice(h * DH, (h + 1) * DH)
        l0, l1, l2 = l0_ref[0, 0, :, sl], l1_scr[h], l2_scr[h]
        mx = jnp.maximum(jnp.maximum(l0, l1), l2)
        w0, w1, w2 = jnp.exp(l0 - mx), jnp.exp(l1 - mx), jnp.exp(l2 - mx)
        merged = (w0 * o0_ref[0, 0, :, sl] + w1 * o1_scr[h] + w2 * o2_scr[h]) / (w0 + w1 + w2)
        heads.append(merged.astype(BF16))
    ydil = jnp.concatenate(heads, axis=-1)

    y = gate(0) * _dot(yhg_ref[...], whg_ref[...])
    y = y + gate(1) * _dot(ydil, wdil_ref[...])
    y = y + gate(2) * _dot(ymem_ref[...], wmem_ref[...])
    z = _dot(y.astype(BF16), wout_ref[...])
    out_ref[...] = h_ref[...] + _rms(z, postw_ref[...])


def _mix(h1, pre_w, w_gate, b_gate, y_hg, dil_outs, y_mem, w_br_hg, w_br_dil, w_br_mem, w_out, post_w, *, seq):
    t = h1.shape[0]
    tm = MIX_TM
    tiles_per_seq = seq // tm
    row_d = pl.BlockSpec((tm, D_MODEL), lambda i: (i, 0))
    row_b = pl.BlockSpec((tm, BR_W), lambda i: (i, 0))

    def perm_spec(d):
        return pl.BlockSpec((1, d, tm // d, BR_W), lambda i: (i // tiles_per_seq, 0, i % tiles_per_seq, 0))

    (o0, l0), (o1, l1), (o2, l2) = dil_outs
    d1, d2 = PERM_DILS
    return pl.pallas_call(
        _mix_kernel,
        grid=(t // tm,),
        in_specs=[
            row_d,
            _const_spec((1, D_MODEL)),
            _const_spec((D_MODEL, N_BRANCH * D_MODEL)),
            _const_spec((1, N_BRANCH * D_MODEL)),
            row_b,
            perm_spec(1), perm_spec(1), perm_spec(d1), perm_spec(d1), perm_spec(d2), perm_spec(d2),
            row_b,
            _const_spec((BR_W, D_MODEL)),
            _const_spec((BR_W, D_MODEL)),
            _const_spec((BR_W, D_MODEL)),
            _const_spec((D_MODEL, D_MODEL)),
            _const_spec((1, D_MODEL)),
        ],
        out_specs=row_d,
        out_shape=jax.ShapeDtypeStruct((t, D_MODEL), F32),
        scratch_shapes=[pltpu.VMEM((HEADS, tm, LANES), F32)] * 4,
        compiler_params=_params("parallel"),
        name="mix",
    )(h1, pre_w, w_gate, b_gate, y_hg, o0, l0, o1, l1, o2, l2, y_mem,
      w_br_hg, w_br_dil, w_br_mem, w_out, post_w)


def _alibi_slopes(group):
    return tuple(2.0 ** (-8.0 * (group * HEADS + h + 1) / ALIBI_HEADS) for h in range(HEADS))


def kernel(x, mem, ffn1_pre_w, ffn1_w_gu, ffn1_w_down, ffn1_post_w, mix_pre_w, w_in, b_gate, hg_lb_logits,
           hg_norm_w, mem_norm_w, w_mem_kv, w_br_hg, w_br_dil, w_br_mem, w_out, mix_post_w, ffn2_pre_w,
           ffn2_w_gu, ffn2_w_down, ffn2_post_w):
    b, s, d = x.shape
    t = b * s
    bf = lambda a: a.astype(BF16)
    depth = ffn1_pre_w.shape[0]
    h = x.reshape(t, d)
    for l in range(depth):
        assert depth == 1
        h, *h_perm = _ffn(h, ffn1_pre_w[l:l + 1], bf(ffn1_w_gu[l]), bf(ffn1_w_down[l]), ffn1_post_w[l:l + 1],
                          seq=s, dils=PERM_DILS)
        h3 = h.reshape(b, s, d)
        h_by_dil = dict(zip(PERM_DILS, h_perm))
        h_by_dil[1] = h.reshape(b, 1, s, d)
        pre_w = mix_pre_w[l:l + 1]
        w_in_l = w_in[l]
        c0 = HG_COLS
        y_hg = _hgrn(h3, pre_w, bf(w_in_l[:, :c0]), hg_lb_logits, hg_norm_w[l:l + 1])
        dil_outs = []
        for g, (_, dil) in enumerate(DIL_GROUPS):
            w_g = bf(w_in_l[:, c0 + g * DIL_GCOLS:c0 + (g + 1) * DIL_GCOLS])
            dil_outs.append(_dil_group(h_by_dil[dil], pre_w, w_g, _alibi_slopes(g)))
        c1 = c0 + len(DIL_GROUPS) * DIL_GCOLS
        mkv = _memkv(mem.reshape(b * N_MEM, d), mem_norm_w[l:l + 1], bf(w_mem_kv[l]))
        y_mem = _memattn(h3, pre_w, bf(w_in_l[:, c1:c1 + BR_W]), mkv.reshape(b, N_MEM, 2 * BR_W))
        c2 = c1 + BR_W
        h = _mix(h, pre_w, bf(w_in_l[:, c2:]), b_gate[l:l + 1], y_hg.reshape(t, BR_W), dil_outs,
                 y_mem.reshape(t, BR_W), bf(w_br_hg[l]), bf(w_br_dil[l]), bf(w_br_mem[l]), bf(w_out[l]),
                 mix_post_w[l:l + 1], seq=s)
        (h,) = _ffn(h, ffn2_pre_w[l:l + 1], bf(ffn2_w_gu[l]), bf(ffn2_w_down[l]), ffn2_post_w[l:l + 1], seq=s)
    return h.reshape(b, s, d)
```

```python
import functools

import jax
import jax.numpy as jnp
import numpy as np
from jax import lax
from jax.experimental import pallas as pl
from jax.experimental.pallas import tpu as pltpu

F32 = jnp.float32
BF16 = jnp.bfloat16

EPS = 1e-6
D_MODEL = 1024
D_FF = 2816
N_MEM = 256
HEADS = 4
DH = 128
LANES = 128
D_SLABS = D_MODEL // LANES
BR_W = HEADS * DH
HG_CHUNK = 64
HG_COLS = 4 * BR_W
DIL_GROUPS = ((128, 1), (512, 4), (2048, 16))
PERM_DILS = tuple(d for _, d in DIL_GROUPS if d > 1)
NK = 128
DIL_GCOLS = 3 * BR_W
N_BRANCH = 3
ATT_SCALE = DH ** -0.5
ALIBI_HEADS = len(DIL_GROUPS) * HEADS

VMEM_LIMIT_BYTES = 56 * 1024 * 1024

FFN_TM = 512
FFN_CK = 256
HG_TS = 512
HG_GROUP = 256
DIL_TQ = 1024
MEM_TQ = 1024
MIX_TM = 512


def _dot(a, b):
    return jnp.dot(a, b, preferred_element_type=F32)


def _dot_nt(a, b):
    return lax.dot_general(a, b, (((1,), (1,)), ((), ())), preferred_element_type=F32)


def _dot_tn(a, b):
    return lax.dot_general(a, b, (((0,), (0,)), ((), ())), preferred_element_type=F32)


def _rms(x, w):
    ms = jnp.mean(x * x, axis=-1, keepdims=True)
    return x * lax.rsqrt(ms + EPS) * w


def _sigmoid(x):
    return 0.5 * jnp.tanh(0.5 * x) + 0.5


def _const_spec(shape):
    zeros = (0,) * len(shape)
    return pl.BlockSpec(shape, lambda *_: zeros, pipeline_mode=pl.Buffered(1))


def _params(*semantics):
    return pltpu.CompilerParams(dimension_semantics=semantics, vmem_limit_bytes=VMEM_LIMIT_BYTES)


def _ffn_tile(h_ref, prew_ref, wgu_ref, wd_ref, postw_ref, a_scr):
    n = _rms(h_ref[...], prew_ref[...]).astype(BF16)
    for c in range(D_FF // FFN_CK):
        lo = c * FFN_CK
        g = _dot(n, wgu_ref[:, lo:lo + FFN_CK])
        u = _dot(n, wgu_ref[:, D_FF + lo:D_FF + lo + FFN_CK])
        a_scr[:, lo:lo + FFN_CK] = (g * _sigmoid(g) * u).astype(BF16)
    z = _dot(a_scr[...], wd_ref[...])
    return h_ref[...] + 0.5 * _rms(z, postw_ref[...])


def _ffn_kernel(h_ref, prew_ref, wgu_ref, wd_ref, postw_ref, o_ref, a_scr):
    o_ref[...] = _ffn_tile(h_ref, prew_ref, wgu_ref, wd_ref, postw_ref, a_scr)


def _ffn_prenorm_kernel(h_ref, prew_ref, wgu_ref, wd_ref, postw_ref, nextw_ref, o_ref, un_ref, p4_ref, p16_ref,
                        a_scr, slab_scr, slab4_scr):
    out = _ffn_tile(h_ref, prew_ref, wgu_ref, wd_ref, postw_ref, a_scr)
    o_ref[...] = out
    un = _rms(out, nextw_ref[...])
    un_ref[...] = un.astype(BF16)
    n4 = FFN_TM // 4
    n16 = FFN_TM // 16
    for c in range(D_SLABS):
        lanes = slice(c * LANES, (c + 1) * LANES)
        slab_scr[c] = un[:, lanes]
        for r4 in range(4):
            rows = slab_scr[c, pl.ds(r4, n4, stride=4), :]
            p4_ref[0, r4, :, lanes] = rows.astype(BF16)
            slab4_scr[c, r4 * n4:(r4 + 1) * n4, :] = rows
        for r4 in range(4):
            for q in range(4):
                rows = slab4_scr[c, pl.ds(r4 * n4 + q, n16, stride=4), :]
                p16_ref[0, r4 + 4 * q, :, lanes] = rows.astype(BF16)


def _ffn(h, pre_w, w_gu, w_down, post_w, next_w=None, *, seq=None, dils=()):
    t = h.shape[0]
    row_spec = pl.BlockSpec((FFN_TM, D_MODEL), lambda i: (i, 0))
    in_specs = [row_spec, _const_spec((1, D_MODEL)), _const_spec((D_MODEL, 2 * D_FF)),
                _const_spec((D_FF, D_MODEL)), _const_spec((1, D_MODEL))]
    args = [h, pre_w, w_gu, w_down, post_w]
    out_specs = [row_spec]
    out_shape = [jax.ShapeDtypeStruct((t, D_MODEL), F32)]
    scratch = [pltpu.VMEM((FFN_TM, D_FF), BF16)]
    kern = _ffn_kernel
    if next_w is not None:
        assert dils == (4, 16)
        tiles_per_seq = seq // FFN_TM
        kern = _ffn_prenorm_kernel
        in_specs.append(_const_spec((1, D_MODEL)))
        args.append(next_w)
        out_specs.append(row_spec)
        out_shape.append(jax.ShapeDtypeStruct((t, D_MODEL), BF16))
        for d in dils:
            out_specs.append(pl.BlockSpec((1, d, FFN_TM // d, D_MODEL),
                                          lambda i: (i // tiles_per_seq, 0, i % tiles_per_seq, 0)))
            out_shape.append(jax.ShapeDtypeStruct((t // seq, d, seq // d, D_MODEL), BF16))
        scratch += [pltpu.VMEM((D_SLABS, FFN_TM, LANES), F32)] * 2
    return pl.pallas_call(
        kern,
        grid=(t // FFN_TM,),
        in_specs=in_specs,
        out_specs=out_specs,
        out_shape=out_shape,
        scratch_shapes=scratch,
        compiler_params=_params("parallel"),
        name="ffn",
    )(*args)


def _hgrn_masks():
    i = np.arange(HG_GROUP)[:, None]
    j = np.arange(HG_GROUP)[None, :]
    same = (i // HG_CHUNK) == (j // HG_CHUNK)
    mid = (i // HG_CHUNK) * HG_CHUNK + HG_CHUNK // 2 - 1
    dmat = same * ((j <= i).astype(np.float32) - (j <= mid).astype(np.float32))
    c = np.arange(16)[:, None]
    jt = np.arange(HG_TS)[None, :]
    smat = ((jt // HG_CHUNK) == c) & (jt % HG_CHUNK < HG_CHUNK // 2)
    return jnp.asarray(dmat, BF16), jnp.asarray(smat.astype(np.float32), BF16)


def _hgrn_kernel(u_ref, w_ref, lbl_ref, nw_ref, dmat_ref, smat_ref, o_ref, proj_scr, oint_scr, st_scr):
    @pl.when(pl.program_id(1) == 0)
    def _():
        st_scr[...] = jnp.zeros_like(st_scr)

    u = u_ref[0]
    proj_scr[:, BR_W:2 * BR_W] = _dot(u, w_ref[:, BR_W:2 * BR_W])
    proj_scr[:, 0:BR_W] = _dot(u, w_ref[:, 0:BR_W])
    proj_scr[:, 2 * BR_W:4 * BR_W] = _dot(u, w_ref[:, 2 * BR_W:4 * BR_W])

    lg = lbl_ref[...]
    e = jnp.exp(lg - jnp.max(lg, axis=0, keepdims=True))
    lb = e[0:1] / jnp.sum(e, axis=0, keepdims=True)

    c_ = HG_CHUNK
    fb = 0.5 * (1.0 - lb)
    ft = fb * jnp.tanh(0.5 * proj_scr[:, BR_W:2 * BR_W])
    f = (lb + fb) + ft
    k = fb - ft
    lf = jnp.log(f)
    lf0 = lf.astype(BF16)
    lf1 = (lf - lf0.astype(F32)).astype(BF16)
    group_rows = [slice(g * HG_GROUP, (g + 1) * HG_GROUP) for g in range(HG_TS // HG_GROUP)]
    bd = jnp.concatenate([_dot(dmat_ref[...], lf0[rows]) + _dot(dmat_ref[...], lf1[rows])
                          for rows in group_rows], axis=0)
    bref = _dot(smat_ref[...], lf0) + _dot(smat_ref[...], lf1)
    qr = proj_scr[:, 0:BR_W]
    qe32 = qr * _sigmoid(qr) * jnp.exp(bd)
    ke32 = k * jnp.exp(-bd)
    qe = qe32.astype(BF16)
    ke = ke32.astype(BF16)
    v = proj_scr[:, 2 * BR_W:3 * BR_W].astype(BF16)

    n_chunks = HG_TS // c_
    qb, kd, dec = [], [], []
    for c in range(n_chunks):
        rows = slice(c * c_, (c + 1) * c_)
        tail = bd[(c + 1) * c_ - 1:(c + 1) * c_]
        brc = bref[c:c + 1]
        kd.append((ke32[rows] * jnp.exp(tail)).astype(BF16))
        qb.append((qe32[rows] * jnp.exp(brc)).astype(BF16))
        dec.append(jnp.exp(brc + tail))

    row = lax.broadcasted_iota(jnp.int32, (HG_GROUP, HG_GROUP), 0)
    col = lax.broadcasted_iota(jnp.int32, (HG_GROUP, HG_GROUP), 1)
    causal = (row >= col) & ((row // c_) == (col // c_))
    nw = nw_ref[...]

    head_sl = [slice(h * DH, (h + 1) * DH) for h in range(HEADS)]
    chunk_rows = [slice(c * c_, (c + 1) * c_) for c in range(n_chunks)]
    kv_t = [[_dot_tn(v[chunk_rows[c], sl], kd[c][:, sl]) for sl in head_sl] for c in range(n_chunks)]
    st = [st_scr[h] for h in range(HEADS)]
    st_in = []
    for c in range(n_chunks):
        st_in.append([s.astype(BF16) for s in st])
        st = [st[h] * dec[c][:, head_sl[h]] + kv_t[c][h] for h in range(HEADS)]
    for h in range(HEADS):
        st_scr[h] = st[h]
    for c in range(n_chunks):
        for h, sl in enumerate(head_sl):
            oint_scr[chunk_rows[c], sl] = _dot_nt(qb[c][:, sl], st_in[c][h])

    for h in range(HEADS):
        sl = head_sl[h]
        for g in range(HG_TS // HG_GROUP):
            rows = slice(g * HG_GROUP, (g + 1) * HG_GROUP)
            sc = jnp.where(causal, _dot_nt(qe[rows, sl], ke[rows, sl]), 0.0).astype(BF16)
            o = _dot(sc, v[rows, sl]) + oint_scr[rows, sl]
            y = _rms(o, nw) * _sigmoid(proj_scr[rows, 3 * BR_W + h * DH:3 * BR_W + (h + 1) * DH])
            o_ref[0, rows, sl] = y.astype(BF16)


def _hgrn(u, w_hg, lb_logits, norm_w):
    b, s, _ = u.shape
    dmat, smat = _hgrn_masks()
    return pl.pallas_call(
        _hgrn_kernel,
        grid=(b, s // HG_TS),
        in_specs=[
            pl.BlockSpec((1, HG_TS, D_MODEL), lambda bi, i: (bi, i, 0)),
            _const_spec((D_MODEL, HG_COLS)),
            _const_spec(lb_logits.shape),
            _const_spec((1, DH)),
            _const_spec(dmat.shape),
            _const_spec(smat.shape),
        ],
        out_specs=pl.BlockSpec((1, HG_TS, BR_W), lambda bi, i: (bi, i, 0)),
        out_shape=jax.ShapeDtypeStruct((b, s, BR_W), BF16),
        scratch_shapes=[pltpu.VMEM((HG_TS, HG_COLS), F32), pltpu.VMEM((HG_TS, BR_W), F32),
                        pltpu.VMEM((HEADS, DH, DH), F32)],
        compiler_params=_params("parallel", "arbitrary"),
        name="hgrn",
    )(u, w_hg, lb_logits, norm_w, dmat, smat)


def _dil_bias(dil, slopes):
    qi = np.arange(NK)[:, None]
    kj = np.arange(2 * NK)[None, :]
    delta = NK + qi - kj
    band = (delta >= 0) & (delta <= NK)
    dist = (delta * dil).astype(np.float32)
    rows = [[np.where(valid, -np.float32(sl) * dist, -np.inf) for sl in slopes]
            for valid in (band, band & (kj >= NK))]
    return jnp.asarray(np.array(rows, np.float32))


def _dil_kernel(u_ref, w_ref, bias_ref, o_ref, lse_ref, q_scr, kv_scr, *, nr, tq):
    i = pl.program_id(2)
    rows = nr * tq

    @pl.when(i == 0)
    def _():
        kv_scr[0:NK, :] = jnp.zeros((NK, 2 * BR_W), BF16)

    @pl.when(i > 0)
    def _():
        kv_scr[0:NK, :] = kv_scr[rows:rows + NK, :]

    qkv = _dot(u_ref[0].reshape(rows, D_MODEL), w_ref[...])
    q_scr[...] = qkv[:, 0:BR_W].astype(BF16)
    kv_scr[NK:NK + rows, :] = qkv[:, BR_W:3 * BR_W].astype(BF16)

    seq_start = jnp.where(i == 0, 1, 0)
    for rr in range(nr):
        for j in range(tq // NK):
            lo = rr * tq + j * NK
            for h in range(HEADS):
                sl = slice(h * DH, (h + 1) * DH)
                q = q_scr[lo:lo + NK, sl]
                kk = kv_scr[lo:lo + 2 * NK, sl]
                vv = kv_scr[lo:lo + 2 * NK, BR_W + h * DH:BR_W + (h + 1) * DH]
                bias = bias_ref[seq_start, h] if j == 0 else bias_ref[0, h]
                s = _dot_nt(q, kk) * ATT_SCALE + bias
                m = jnp.max(s, axis=-1, keepdims=True)
                p = jnp.exp(s - m)
                l = jnp.sum(p, axis=-1, keepdims=True)
                o = _dot(p.astype(BF16), vv)
                o_ref[0, rr, j * NK:(j + 1) * NK, sl] = o / l
                lse_ref[0, rr, j * NK:(j + 1) * NK, sl] = jnp.broadcast_to(m + jnp.log(l), (NK, DH))


def _dil_group(up, w_g, dil_bias):
    b, dil, l, _ = up.shape
    tq = min(DIL_TQ, l)
    nr = DIL_TQ // tq
    kern = functools.partial(_dil_kernel, nr=nr, tq=tq)
    row_spec = pl.BlockSpec((1, nr, tq, BR_W), lambda bi, r, i: (bi, r, i, 0))
    return pl.pallas_call(
        kern,
        grid=(b, dil // nr, l // tq),
        in_specs=[
            pl.BlockSpec((1, nr, tq, D_MODEL), lambda bi, r, i: (bi, r, i, 0)),
            _const_spec((D_MODEL, DIL_GCOLS)),
            _const_spec(dil_bias.shape),
        ],
        out_specs=[row_spec, row_spec],
        out_shape=[jax.ShapeDtypeStruct((b, dil, l, BR_W), F32)] * 2,
        scratch_shapes=[pltpu.VMEM((nr * tq, BR_W), BF16), pltpu.VMEM((NK + nr * tq, 2 * BR_W), BF16)],
        compiler_params=_params("parallel", "parallel", "arbitrary"),
        name=f"dil{dil}",
    )(up, w_g, dil_bias)


def _memkv_kernel(m_ref, nw_ref, w_ref, o_ref):
    n = _rms(m_ref[...], nw_ref[...]).astype(BF16)
    o_ref[...] = _dot(n, w_ref[...]).astype(BF16)


def _memkv(mem2d, norm_w, w_kv):
    t = mem2d.shape[0]
    tm = 512
    return pl.pallas_call(
        _memkv_kernel,
        grid=(t // tm,),
        in_specs=[
            pl.BlockSpec((tm, D_MODEL), lambda i: (i, 0)),
            _const_spec((1, D_MODEL)),
            _const_spec((D_MODEL, 2 * BR_W)),
        ],
        out_specs=pl.BlockSpec((tm, 2 * BR_W), lambda i: (i, 0)),
        out_shape=jax.ShapeDtypeStruct((t, 2 * BR_W), BF16),
        compiler_params=_params("parallel"),
        name="memkv",
    )(mem2d, norm_w, w_kv)


def _memattn_kernel(u_ref, w_ref, kv_ref, o_ref):
    mq = _dot(u_ref[0], w_ref[...]).astype(BF16)
    for h in range(HEADS):
        sl = slice(h * DH, (h + 1) * DH)
        s = _dot_nt(mq[:, sl], kv_ref[0, :, sl]) * ATT_SCALE
        e = jnp.exp(s - jnp.max(s, axis=-1, keepdims=True))
        p = e / jnp.sum(e, axis=-1, keepdims=True)
        o = _dot(p.astype(BF16), kv_ref[0, :, BR_W + h * DH:BR_W + (h + 1) * DH])
        o_ref[0, :, sl] = o.astype(BF16)


def _memattn(u, w_mq, mkv):
    b, s, _ = u.shape
    return pl.pallas_call(
        _memattn_kernel,
        grid=(b, s // MEM_TQ),
        in_specs=[
            pl.BlockSpec((1, MEM_TQ, D_MODEL), lambda bi, i: (bi, i, 0)),
            _const_spec((D_MODEL, BR_W)),
            pl.BlockSpec((1, N_MEM, 2 * BR_W), lambda bi, i: (bi, 0, 0)),
        ],
        out_specs=pl.BlockSpec((1, MEM_TQ, BR_W), lambda bi, i: (bi, i, 0)),
        out_shape=jax.ShapeDtypeStruct((b, s, BR_W), BF16),
        compiler_params=_params("parallel", "parallel"),
        name="memattn",
    )(u, w_mq, mkv)


def _mix_kernel(h_ref, u_ref, wg_ref, bg_ref, yhg_ref, o0_ref, l0_ref, o1_ref, l1_ref, o2_ref, l2_ref,
                ymem_ref, whg_ref, wdil_ref, wmem_ref, wout_ref, postw_ref, out_ref, *slab_scrs):
    u = u_ref[...]

    def gate(bidx):
        sl = slice(bidx * D_MODEL, (bidx + 1) * D_MODEL)
        return _sigmoid(_dot(u, wg_ref[:, sl]) + bg_ref[:, sl])

    o1_scr, l1_scr, o2_scr, l2_scr, tmp_scr = slab_scrs
    n4 = MIX_TM // 4
    n16 = MIX_TM // 16
    for src_ref, scr in ((o1_ref, o1_scr), (l1_ref, l1_scr)):
        for r4 in range(4):
            for h in range(HEADS):
                scr[h, pl.ds(r4, n4, stride=4), :] = src_ref[0, r4, :, h * DH:(h + 1) * DH]
    for src_ref, scr in ((o2_ref, o2_scr), (l2_ref, l2_scr)):
        for h in range(HEADS):
            for r4 in range(4):
                for q in range(4):
                    tmp_scr[h, pl.ds(r4 * n4 + q, n16, stride=4), :] = src_ref[0, r4 + 4 * q, :, h * DH:(h + 1) * DH]
            for r4 in range(4):
                scr[h, pl.ds(r4, n4, stride=4), :] = tmp_scr[h, r4 * n4:(r4 + 1) * n4, :]

    heads = []
    for h in range(HEADS):
        sl = slice(h * DH, (h + 1) * DH)
        l0, l1, l2 = l0_ref[0, 0, :, sl], l1_scr[h], l2_scr[h]
        mx = jnp.maximum(jnp.maximum(l0, l1), l2)
        w0, w1, w2 = jnp.exp(l0 - mx), jnp.exp(l1 - mx), jnp.exp(l2 - mx)
        merged = (w0 * o0_ref[0, 0, :, sl] + w1 * o1_scr[h] + w2 * o2_scr[h]) / (w0 + w1 + w2)
        heads.append(merged.astype(BF16))
    ydil = jnp.concatenate(heads, axis=-1)

    y = gate(0) * _dot(yhg_ref[...], whg_ref[...])
    y = y + gate(1) * _dot(ydil, wdil_ref[...])
    y = y + gate(2) * _dot(ymem_ref[...], wmem_ref[...])
    z = _dot(y.astype(BF16), wout_ref[...])
    out_ref[...] = h_ref[...] + _rms(z, postw_ref[...])


def _mix(h1, u, w_gate, b_gate, y_hg, dil_outs, y_mem, w_br_hg, w_br_dil, w_br_mem, w_out, post_w, *, seq):
    t = h1.shape[0]
    tm = MIX_TM
    tiles_per_seq = seq // tm
    row_d = pl.BlockSpec((tm, D_MODEL), lambda i: (i, 0))
    row_b = pl.BlockSpec((tm, BR_W), lambda i: (i, 0))

    def perm_spec(d):
        return pl.BlockSpec((1, d, tm // d, BR_W), lambda i: (i // tiles_per_seq, 0, i % tiles_per_seq, 0))

    (o0, l0), (o1, l1), (o2, l2) = dil_outs
    d1, d2 = PERM_DILS
    return pl.pallas_call(
        _mix_kernel,
        grid=(t // tm,),
        in_specs=[
            row_d,
            row_d,
            _const_spec((D_MODEL, N_BRANCH * D_MODEL)),
            _const_spec((1, N_BRANCH * D_MODEL)),
            row_b,
            perm_spec(1), perm_spec(1), perm_spec(d1), perm_spec(d1), perm_spec(d2), perm_spec(d2),
            row_b,
            _const_spec((BR_W, D_MODEL)),
            _const_spec((BR_W, D_MODEL)),
            _const_spec((BR_W, D_MODEL)),
            _const_spec((D_MODEL, D_MODEL)),
            _const_spec((1, D_MODEL)),
        ],
        out_specs=row_d,
        out_shape=jax.ShapeDtypeStruct((t, D_MODEL), F32),
        scratch_shapes=[pltpu.VMEM((HEADS, tm, LANES), F32)] * 5,
        compiler_params=_params("parallel"),
        name="mix",
    )(h1, u, w_gate, b_gate, y_hg, o0, l0, o1, l1, o2, l2, y_mem,
      w_br_hg, w_br_dil, w_br_mem, w_out, post_w)


def _alibi_slopes(group):
    return tuple(2.0 ** (-8.0 * (group * HEADS + h + 1) / ALIBI_HEADS) for h in range(HEADS))


def kernel(x, mem, ffn1_pre_w, ffn1_w_gu, ffn1_w_down, ffn1_post_w, mix_pre_w, w_in, b_gate, hg_lb_logits,
           hg_norm_w, mem_norm_w, w_mem_kv, w_br_hg, w_br_dil, w_br_mem, w_out, mix_post_w, ffn2_pre_w,
           ffn2_w_gu, ffn2_w_down, ffn2_post_w):
    b, s, d = x.shape
    t = b * s
    bf = lambda a: a.astype(BF16)
    depth = ffn1_pre_w.shape[0]
    h = x.reshape(t, d)
    for l in range(depth):
        assert depth == 1
        h, u, *u_perm = _ffn(h, ffn1_pre_w[l:l + 1], bf(ffn1_w_gu[l]), bf(ffn1_w_down[l]), ffn1_post_w[l:l + 1],
                             mix_pre_w[l:l + 1], seq=s, dils=PERM_DILS)
        u3 = u.reshape(b, s, d)
        u_by_dil = dict(zip(PERM_DILS, u_perm))
        u_by_dil[1] = u.reshape(b, 1, s, d)
        w_in_l = w_in[l]
        c0 = HG_COLS
        y_hg = _hgrn(u3, bf(w_in_l[:, :c0]), hg_lb_logits, hg_norm_w[l:l + 1])
        dil_outs = []
        for g, (_, dil) in enumerate(DIL_GROUPS):
            w_g = bf(w_in_l[:, c0 + g * DIL_GCOLS:c0 + (g + 1) * DIL_GCOLS])
            dil_outs.append(_dil_group(u_by_dil[dil], w_g, _dil_bias(dil, _alibi_slopes(g))))
        c1 = c0 + len(DIL_GROUPS) * DIL_GCOLS
        mkv = _memkv(mem.reshape(b * N_MEM, d), mem_norm_w[l:l + 1], bf(w_mem_kv[l]))
        y_mem = _memattn(u3, bf(w_in_l[:, c1:c1 + BR_W]), mkv.reshape(b, N_MEM, 2 * BR_W))
        c2 = c1 + BR_W
        h = _mix(h, u, bf(w_in_l[:, c2:]), b_gate[l:l + 1], y_hg.reshape(t, BR_W), dil_outs,
                 y_mem.reshape(t, BR_W), bf(w_br_hg[l]), bf(w_br_dil[l]), bf(w_br_mem[l]), bf(w_out[l]),
                 mix_post_w[l:l + 1], seq=s)
        (h,) = _ffn(h, ffn2_pre_w[l:l + 1], bf(ffn2_w_gu[l]), bf(ffn2_w_down[l]), ffn2_post_w[l:l + 1])
    return h.reshape(b, s, d)
```

```python
import functools

import jax
import jax.numpy as jnp
import numpy as np
from jax import lax
from jax.experimental import pallas as pl
from jax.experimental.pallas import tpu as pltpu

F32 = jnp.float32
BF16 = jnp.bfloat16

EPS = 1e-6
D_MODEL = 1024
D_FF = 2816
N_MEM = 256
HEADS = 4
DH = 128
LANES = 128
D_SLABS = D_MODEL // LANES
BR_W = HEADS * DH
HG_CHUNK = 64
HG_COLS = 4 * BR_W
DIL_GROUPS = ((128, 1), (512, 4), (2048, 16))
PERM_DILS = tuple(d for _, d in DIL_GROUPS if d > 1)
NK = 128
DIL_GCOLS = 3 * BR_W
N_BRANCH = 3
ATT_SCALE = DH ** -0.5
ALIBI_HEADS = len(DIL_GROUPS) * HEADS

VMEM_LIMIT_BYTES = 56 * 1024 * 1024

FFN_TM = 512
FFN_CK = 256
HG_TS = 512
HG_GROUP = 256
DIL_TQ = 1024
MEM_TQ = 1024
MIX_TM = 512


def _dot(a, b):
    return jnp.dot(a, b, preferred_element_type=F32)


def _dot_nt(a, b):
    return lax.dot_general(a, b, (((1,), (1,)), ((), ())), preferred_element_type=F32)


def _dot_tn(a, b):
    return lax.dot_general(a, b, (((0,), (0,)), ((), ())), preferred_element_type=F32)


def _rms(x, w):
    ms = jnp.mean(x * x, axis=-1, keepdims=True)
    return x * lax.rsqrt(ms + EPS) * w


def _sigmoid(x):
    return 0.5 * jnp.tanh(0.5 * x) + 0.5


def _const_spec(shape):
    zeros = (0,) * len(shape)
    return pl.BlockSpec(shape, lambda *_: zeros, pipeline_mode=pl.Buffered(1))


def _params(*semantics):
    return pltpu.CompilerParams(dimension_semantics=semantics, vmem_limit_bytes=VMEM_LIMIT_BYTES)


def _ffn_tile(h_ref, prew_ref, wgu_ref, wd_ref, postw_ref, a_scr):
    n = _rms(h_ref[...], prew_ref[...]).astype(BF16)
    for c in range(D_FF // FFN_CK):
        lo = c * FFN_CK
        g = _dot(n, wgu_ref[:, lo:lo + FFN_CK])
        u = _dot(n, wgu_ref[:, D_FF + lo:D_FF + lo + FFN_CK])
        a_scr[:, lo:lo + FFN_CK] = (g * _sigmoid(g) * u).astype(BF16)
    z = _dot(a_scr[...], wd_ref[...])
    return h_ref[...] + 0.5 * _rms(z, postw_ref[...])


def _ffn_kernel(h_ref, prew_ref, wgu_ref, wd_ref, postw_ref, o_ref, a_scr):
    o_ref[...] = _ffn_tile(h_ref, prew_ref, wgu_ref, wd_ref, postw_ref, a_scr)


def _ffn_prenorm_kernel(h_ref, prew_ref, wgu_ref, wd_ref, postw_ref, nextw_ref, o_ref, un_ref, p4_ref, p16_ref,
                        a_scr, slab_scr, slab4_scr):
    out = _ffn_tile(h_ref, prew_ref, wgu_ref, wd_ref, postw_ref, a_scr)
    o_ref[...] = out
    un = _rms(out, nextw_ref[...])
    un_ref[...] = un.astype(BF16)
    n4 = FFN_TM // 4
    n16 = FFN_TM // 16
    for c in range(D_SLABS):
        lanes = slice(c * LANES, (c + 1) * LANES)
        slab_scr[c] = un[:, lanes]
        for r4 in range(4):
            rows = slab_scr[c, pl.ds(r4, n4, stride=4), :]
            p4_ref[0, r4, :, lanes] = rows.astype(BF16)
            slab4_scr[c, r4 * n4:(r4 + 1) * n4, :] = rows
        for r4 in range(4):
            for q in range(4):
                rows = slab4_scr[c, pl.ds(r4 * n4 + q, n16, stride=4), :]
                p16_ref[0, r4 + 4 * q, :, lanes] = rows.astype(BF16)


def _ffn(h, pre_w, w_gu, w_down, post_w, next_w=None, *, seq=None, dils=()):
    t = h.shape[0]
    row_spec = pl.BlockSpec((FFN_TM, D_MODEL), lambda i: (i, 0))
    in_specs = [row_spec, _const_spec((1, D_MODEL)), _const_spec((D_MODEL, 2 * D_FF)),
                _const_spec((D_FF, D_MODEL)), _const_spec((1, D_MODEL))]
    args = [h, pre_w, w_gu, w_down, post_w]
    out_specs = [row_spec]
    out_shape = [jax.ShapeDtypeStruct((t, D_MODEL), F32)]
    scratch = [pltpu.VMEM((FFN_TM, D_FF), BF16)]
    kern = _ffn_kernel
    if next_w is not None:
        assert dils == (4, 16)
        tiles_per_seq = seq // FFN_TM
        kern = _ffn_prenorm_kernel
        in_specs.append(_const_spec((1, D_MODEL)))
        args.append(next_w)
        out_specs.append(row_spec)
        out_shape.append(jax.ShapeDtypeStruct((t, D_MODEL), BF16))
        for d in dils:
            out_specs.append(pl.BlockSpec((1, d, FFN_TM // d, D_MODEL),
                                          lambda i: (i // tiles_per_seq, 0, i % tiles_per_seq, 0)))
            out_shape.append(jax.ShapeDtypeStruct((t // seq, d, seq // d, D_MODEL), BF16))
        scratch += [pltpu.VMEM((D_SLABS, FFN_TM, LANES), F32)] * 2
    return pl.pallas_call(
        kern,
        grid=(t // FFN_TM,),
        in_specs=in_specs,
        out_specs=out_specs,
        out_shape=out_shape,
        scratch_shapes=scratch,
        compiler_params=_params("parallel"),
        name="ffn",
    )(*args)


def _hgrn_masks():
    i = np.arange(HG_GROUP)[:, None]
    j = np.arange(HG_GROUP)[None, :]
    same = (i // HG_CHUNK) == (j // HG_CHUNK)
    mid = (i // HG_CHUNK) * HG_CHUNK + HG_CHUNK // 2 - 1
    dmat = same * ((j <= i).astype(np.float32) - (j <= mid).astype(np.float32))
    c = np.arange(16)[:, None]
    jt = np.arange(HG_TS)[None, :]
    smat = ((jt // HG_CHUNK) == c) & (jt % HG_CHUNK < HG_CHUNK // 2)
    return jnp.asarray(dmat, BF16), jnp.asarray(smat.astype(np.float32), BF16)


def _hgrn_kernel(u_ref, w_ref, lbl_ref, nw_ref, dmat_ref, smat_ref, o_ref, proj_scr, oint_scr, st_scr):
    @pl.when(pl.program_id(1) == 0)
    def _():
        st_scr[...] = jnp.zeros_like(st_scr)

    u = u_ref[0]
    proj_scr[:, BR_W:2 * BR_W] = _dot(u, w_ref[:, BR_W:2 * BR_W])
    proj_scr[:, 0:BR_W] = _dot(u, w_ref[:, 0:BR_W])
    proj_scr[:, 2 * BR_W:4 * BR_W] = _dot(u, w_ref[:, 2 * BR_W:4 * BR_W])

    lg = lbl_ref[...]
    e = jnp.exp(lg - jnp.max(lg, axis=0, keepdims=True))
    lb = e[0:1] / jnp.sum(e, axis=0, keepdims=True)

    c_ = HG_CHUNK
    fb = 0.5 * (1.0 - lb)
    ft = fb * jnp.tanh(0.5 * proj_scr[:, BR_W:2 * BR_W])
    f = (lb + fb) + ft
    k = fb - ft
    lf = jnp.log(f)
    lf0 = lf.astype(BF16)
    lf1 = (lf - lf0.astype(F32)).astype(BF16)
    group_rows = [slice(g * HG_GROUP, (g + 1) * HG_GROUP) for g in range(HG_TS // HG_GROUP)]
    bd = jnp.concatenate([_dot(dmat_ref[...], lf0[rows]) + _dot(dmat_ref[...], lf1[rows])
                          for rows in group_rows], axis=0)
    bref = _dot(smat_ref[...], lf0) + _dot(smat_ref[...], lf1)
    qr = proj_scr[:, 0:BR_W]
    qe32 = qr * _sigmoid(qr) * jnp.exp(bd)
    ke32 = k * jnp.exp(-bd)
    qe = qe32.astype(BF16)
    ke = ke32.astype(BF16)
    v = proj_scr[:, 2 * BR_W:3 * BR_W].astype(BF16)

    n_chunks = HG_TS // c_
    qb, kd, dec = [], [], []
    for c in range(n_chunks):
        rows = slice(c * c_, (c + 1) * c_)
        tail = bd[(c + 1) * c_ - 1:(c + 1) * c_]
        brc = bref[c:c + 1]
        kd.append((ke32[rows] * jnp.exp(tail)).astype(BF16))
        qb.append((qe32[rows] * jnp.exp(brc)).astype(BF16))
        dec.append(jnp.exp(brc + tail))

    row = lax.broadcasted_iota(jnp.int32, (HG_GROUP, HG_GROUP), 0)
    col = lax.broadcasted_iota(jnp.int32, (HG_GROUP, HG_GROUP), 1)
    causal = (row >= col) & ((row // c_) == (col // c_))
    nw = nw_ref[...]

    head_sl = [slice(h * DH, (h + 1) * DH) for h in range(HEADS)]
    chunk_rows = [slice(c * c_, (c + 1) * c_) for c in range(n_chunks)]
    kv_t = [[_dot_tn(v[chunk_rows[c], sl], kd[c][:, sl]) for sl in head_sl] for c in range(n_chunks)]
    st = [st_scr[h] for h in range(HEADS)]
    st_in = []
    for c in range(n_chunks):
        st_in.append([s.astype(BF16) for s in st])
        st = [st[h] * dec[c][:, head_sl[h]] + kv_t[c][h] for h in range(HEADS)]
    for h in range(HEADS):
        st_scr[h] = st[h]
    for c in range(n_chunks):
        for h, sl in enumerate(head_sl):
            oint_scr[chunk_rows[c], sl] = _dot_nt(qb[c][:, sl], st_in[c][h])

    for h in range(HEADS):
        sl = head_sl[h]
        for g in range(HG_TS // HG_GROUP):
            rows = slice(g * HG_GROUP, (g + 1) * HG_GROUP)
            sc = jnp.where(causal, _dot_nt(qe[rows, sl], ke[rows, sl]), 0.0).astype(BF16)
            o = _dot(sc, v[rows, sl]) + oint_scr[rows, sl]
            y = _rms(o, nw) * _sigmoid(proj_scr[rows, 3 * BR_W + h * DH:3 * BR_W + (h + 1) * DH])
            o_ref[0, rows, sl] = y.astype(BF16)


def _hgrn(u, w_hg, lb_logits, norm_w):
    b, s, _ = u.shape
    dmat, smat = _hgrn_masks()
    return pl.pallas_call(
        _hgrn_kernel,
        grid=(b, s // HG_TS),
        in_specs=[
            pl.BlockSpec((1, HG_TS, D_MODEL), lambda bi, i: (bi, i, 0)),
            _const_spec((D_MODEL, HG_COLS)),
            _const_spec(lb_logits.shape),
            _const_spec((1, DH)),
            _const_spec(dmat.shape),
            _const_spec(smat.shape),
        ],
        out_specs=pl.BlockSpec((1, HG_TS, BR_W), lambda bi, i: (bi, i, 0)),
        out_shape=jax.ShapeDtypeStruct((b, s, BR_W), BF16),
        scratch_shapes=[pltpu.VMEM((HG_TS, HG_COLS), F32), pltpu.VMEM((HG_TS, BR_W), F32),
                        pltpu.VMEM((HEADS, DH, DH), F32)],
        compiler_params=_params("parallel", "arbitrary"),
        name="hgrn",
    )(u, w_hg, lb_logits, norm_w, dmat, smat)


def _dil_bias(dil, slopes):
    qi = np.arange(NK)[:, None]
    kj = np.arange(2 * NK)[None, :]
    delta = NK + qi - kj
    band = (delta >= 0) & (delta <= NK)
    dist = (delta * dil).astype(np.float32)
    rows = [[np.where(valid, -np.float32(sl) * dist, -np.inf) for sl in slopes]
            for valid in (band, band & (kj >= NK))]
    return jnp.asarray(np.array(rows, np.float32))


def _dil_kernel(u_ref, w_ref, bias_ref, o_ref, lse_ref, q_scr, kv_scr, *, nr, tq):
    i = pl.program_id(2)
    rows = nr * tq

    @pl.when(i == 0)
    def _():
        kv_scr[0:NK, :] = jnp.zeros((NK, 2 * BR_W), BF16)

    @pl.when(i > 0)
    def _():
        kv_scr[0:NK, :] = kv_scr[rows:rows + NK, :]

    qkv = _dot(u_ref[0].reshape(rows, D_MODEL), w_ref[...])
    q_scr[...] = qkv[:, 0:BR_W].astype(BF16)
    kv_scr[NK:NK + rows, :] = qkv[:, BR_W:3 * BR_W].astype(BF16)

    seq_start = jnp.where(i == 0, 1, 0)
    lane = lax.broadcasted_iota(jnp.int32, (NK, LANES), 1)
    for rr in range(nr):
        for j in range(tq // NK):
            lo = rr * tq + j * NK
            lse_tile = jnp.zeros((NK, LANES), F32)
            for h in range(HEADS):
                sl = slice(h * DH, (h + 1) * DH)
                q = q_scr[lo:lo + NK, sl]
                kk = kv_scr[lo:lo + 2 * NK, sl]
                vv = kv_scr[lo:lo + 2 * NK, BR_W + h * DH:BR_W + (h + 1) * DH]
                bias = bias_ref[seq_start, h] if j == 0 else bias_ref[0, h]
                s = _dot_nt(q, kk) * ATT_SCALE + bias
                m = jnp.max(s, axis=-1, keepdims=True)
                p = jnp.exp(s - m)
                l = jnp.sum(p, axis=-1, keepdims=True)
                o = _dot(p.astype(BF16), vv)
                o_ref[0, rr, j * NK:(j + 1) * NK, sl] = (o / l).astype(BF16)
                lse_tile = jnp.where(lane == h, m + jnp.log(l), lse_tile)
            lse_ref[0, rr, j * NK:(j + 1) * NK, :] = lse_tile


def _dil_group(up, w_g, dil_bias):
    b, dil, l, _ = up.shape
    tq = min(DIL_TQ, l)
    nr = DIL_TQ // tq
    kern = functools.partial(_dil_kernel, nr=nr, tq=tq)
    return pl.pallas_call(
        kern,
        grid=(b, dil // nr, l // tq),
        in_specs=[
            pl.BlockSpec((1, nr, tq, D_MODEL), lambda bi, r, i: (bi, r, i, 0)),
            _const_spec((D_MODEL, DIL_GCOLS)),
            _const_spec(dil_bias.shape),
        ],
        out_specs=[pl.BlockSpec((1, nr, tq, BR_W), lambda bi, r, i: (bi, r, i, 0)),
                   pl.BlockSpec((1, nr, tq, LANES), lambda bi, r, i: (bi, r, i, 0))],
        out_shape=[jax.ShapeDtypeStruct((b, dil, l, BR_W), BF16),
                   jax.ShapeDtypeStruct((b, dil, l, LANES), F32)],
        scratch_shapes=[pltpu.VMEM((nr * tq, BR_W), BF16), pltpu.VMEM((NK + nr * tq, 2 * BR_W), BF16)],
        compiler_params=_params("parallel", "parallel", "arbitrary"),
        name=f"dil{dil}",
    )(up, w_g, dil_bias)


def _memkv_kernel(m_ref, nw_ref, w_ref, o_ref):
    n = _rms(m_ref[...], nw_ref[...]).astype(BF16)
    o_ref[...] = _dot(n, w_ref[...]).astype(BF16)


def _memkv(mem2d, norm_w, w_kv):
    t = mem2d.shape[0]
    tm = 512
    return pl.pallas_call(
        _memkv_kernel,
        grid=(t // tm,),
        in_specs=[
            pl.BlockSpec((tm, D_MODEL), lambda i: (i, 0)),
            _const_spec((1, D_MODEL)),
            _const_spec((D_MODEL, 2 * BR_W)),
        ],
        out_specs=pl.BlockSpec((tm, 2 * BR_W), lambda i: (i, 0)),
        out_shape=jax.ShapeDtypeStruct((t, 2 * BR_W), BF16),
        compiler_params=_params("parallel"),
        name="memkv",
    )(mem2d, norm_w, w_kv)


def _memattn_kernel(u_ref, w_ref, kv_ref, o_ref):
    mq = _dot(u_ref[0], w_ref[...]).astype(BF16)
    for h in range(HEADS):
        sl = slice(h * DH, (h + 1) * DH)
        s = _dot_nt(mq[:, sl], kv_ref[0, :, sl]) * ATT_SCALE
        e = jnp.exp(s - jnp.max(s, axis=-1, keepdims=True))
        p = e / jnp.sum(e, axis=-1, keepdims=True)
        o = _dot(p.astype(BF16), kv_ref[0, :, BR_W + h * DH:BR_W + (h + 1) * DH])
        o_ref[0, :, sl] = o.astype(BF16)


def _memattn(u, w_mq, mkv):
    b, s, _ = u.shape
    return pl.pallas_call(
        _memattn_kernel,
        grid=(b, s // MEM_TQ),
        in_specs=[
            pl.BlockSpec((1, MEM_TQ, D_MODEL), lambda bi, i: (bi, i, 0)),
            _const_spec((D_MODEL, BR_W)),
            pl.BlockSpec((1, N_MEM, 2 * BR_W), lambda bi, i: (bi, 0, 0)),
        ],
        out_specs=pl.BlockSpec((1, MEM_TQ, BR_W), lambda bi, i: (bi, i, 0)),
        out_shape=jax.ShapeDtypeStruct((b, s, BR_W), BF16),
        compiler_params=_params("parallel", "parallel"),
        name="memattn",
    )(u, w_mq, mkv)


def _mix_kernel(h_ref, u_ref, wg_ref, bg_ref, yhg_ref, o0_ref, l0_ref, o1_ref, l1_ref, o2_ref, l2_ref,
                ymem_ref, whg_ref, wdil_ref, wmem_ref, wout_ref, postw_ref, out_ref,
                o1_scr, o2_scr, otmp_scr, l1_scr, l2_scr, ltmp_scr):
    u = u_ref[...]

    def gate(bidx):
        sl = slice(bidx * D_MODEL, (bidx + 1) * D_MODEL)
        return _sigmoid(_dot(u, wg_ref[:, sl]) + bg_ref[:, sl])

    n4 = MIX_TM // 4
    n16 = MIX_TM // 16
    for r4 in range(4):
        l1_scr[pl.ds(r4, n4, stride=4), :] = l1_ref[0, r4]
        for h in range(HEADS):
            o1_scr[h, pl.ds(r4, n4, stride=4), :] = o1_ref[0, r4, :, h * DH:(h + 1) * DH].astype(F32)
    for r4 in range(4):
        for q in range(4):
            ltmp_scr[pl.ds(r4 * n4 + q, n16, stride=4), :] = l2_ref[0, r4 + 4 * q]
        l2_scr[pl.ds(r4, n4, stride=4), :] = ltmp_scr[r4 * n4:(r4 + 1) * n4, :]
    for h in range(HEADS):
        for r4 in range(4):
            for q in range(4):
                otmp_scr[h, pl.ds(r4 * n4 + q, n16, stride=4), :] = (
                    o2_ref[0, r4 + 4 * q, :, h * DH:(h + 1) * DH].astype(F32))
        for r4 in range(4):
            o2_scr[h, pl.ds(r4, n4, stride=4), :] = otmp_scr[h, r4 * n4:(r4 + 1) * n4, :]

    l0, l1, l2 = l0_ref[0, 0], l1_scr[...], l2_scr[...]
    mx = jnp.maximum(jnp.maximum(l0, l1), l2)
    e0, e1, e2 = jnp.exp(l0 - mx), jnp.exp(l1 - mx), jnp.exp(l2 - mx)
    inv = 1.0 / (e0 + e1 + e2)
    w0, w1, w2 = e0 * inv, e1 * inv, e2 * inv
    heads = []
    for h in range(HEADS):
        sl = slice(h * DH, (h + 1) * DH)
        b0, b1, b2 = (jnp.broadcast_to(w[:, h:h + 1], (MIX_TM, DH)) for w in (w0, w1, w2))
        merged = b0 * o0_ref[0, 0, :, sl].astype(F32) + b1 * o1_scr[h] + b2 * o2_scr[h]
        heads.append(merged.astype(BF16))
    ydil = jnp.concatenate(heads, axis=-1)

    y = gate(0) * _dot(yhg_ref[...], whg_ref[...])
    y = y + gate(1) * _dot(ydil, wdil_ref[...])
    y = y + gate(2) * _dot(ymem_ref[...], wmem_ref[...])
    z = _dot(y.astype(BF16), wout_ref[...])
    out_ref[...] = h_ref[...] + _rms(z, postw_ref[...])


def _mix(h1, u, w_gate, b_gate, y_hg, dil_outs, y_mem, w_br_hg, w_br_dil, w_br_mem, w_out, post_w, *, seq):
    t = h1.shape[0]
    tm = MIX_TM
    tiles_per_seq = seq // tm
    row_d = pl.BlockSpec((tm, D_MODEL), lambda i: (i, 0))
    row_b = pl.BlockSpec((tm, BR_W), lambda i: (i, 0))

    def perm_specs(d):
        index = lambda i: (i // tiles_per_seq, 0, i % tiles_per_seq, 0)
        return [pl.BlockSpec((1, d, tm // d, BR_W), index), pl.BlockSpec((1, d, tm // d, LANES), index)]

    (o0, l0), (o1, l1), (o2, l2) = dil_outs
    d1, d2 = PERM_DILS
    assert (d1, d2) == (4, 16)
    return pl.pallas_call(
        _mix_kernel,
        grid=(t // tm,),
        in_specs=[
            row_d,
            row_d,
            _const_spec((D_MODEL, N_BRANCH * D_MODEL)),
            _const_spec((1, N_BRANCH * D_MODEL)),
            row_b,
            *perm_specs(1), *perm_specs(d1), *perm_specs(d2),
            row_b,
            _const_spec((BR_W, D_MODEL)),
            _const_spec((BR_W, D_MODEL)),
            _const_spec((BR_W, D_MODEL)),
            _const_spec((D_MODEL, D_MODEL)),
            _const_spec((1, D_MODEL)),
        ],
        out_specs=row_d,
        out_shape=jax.ShapeDtypeStruct((t, D_MODEL), F32),
        scratch_shapes=[pltpu.VMEM((HEADS, tm, LANES), F32)] * 3 + [pltpu.VMEM((tm, LANES), F32)] * 3,
        compiler_params=_params("parallel"),
        name="mix",
    )(h1, u, w_gate, b_gate, y_hg, o0, l0, o1, l1, o2, l2, y_mem,
      w_br_hg, w_br_dil, w_br_mem, w_out, post_w)


def _alibi_slopes(group):
    return tuple(2.0 ** (-8.0 * (group * HEADS + h + 1) / ALIBI_HEADS) for h in range(HEADS))


def kernel(x, mem, ffn1_pre_w, ffn1_w_gu, ffn1_w_down, ffn1_post_w, mix_pre_w, w_in, b_gate, hg_lb_logits,
           hg_norm_w, mem_norm_w, w_mem_kv, w_br_hg, w_br_dil, w_br_mem, w_out, mix_post_w, ffn2_pre_w,
           ffn2_w_gu, ffn2_w_down, ffn2_post_w):
    b, s, d = x.shape
    t = b * s
    bf = lambda a: a.astype(BF16)
    depth = ffn1_pre_w.shape[0]
    h = x.reshape(t, d)
    for l in range(depth):
        assert depth == 1
        h, u, *u_perm = _ffn(h, ffn1_pre_w[l:l + 1], bf(ffn1_w_gu[l]), bf(ffn1_w_down[l]), ffn1_post_w[l:l + 1],
                             mix_pre_w[l:l + 1], seq=s, dils=PERM_DILS)
        u3 = u.reshape(b, s, d)
        u_by_dil = dict(zip(PERM_DILS, u_perm))
        u_by_dil[1] = u.reshape(b, 1, s, d)
        w_in_l = w_in[l]
        c0 = HG_COLS
        y_hg = _hgrn(u3, bf(w_in_l[:, :c0]), hg_lb_logits, hg_norm_w[l:l + 1])
        dil_outs = []
        for g, (_, dil) in enumerate(DIL_GROUPS):
            w_g = bf(w_in_l[:, c0 + g * DIL_GCOLS:c0 + (g + 1) * DIL_GCOLS])
            dil_outs.append(_dil_group(u_by_dil[dil], w_g, _dil_bias(dil, _alibi_slopes(g))))
        c1 = c0 + len(DIL_GROUPS) * DIL_GCOLS
        mkv = _memkv(mem.reshape(b * N_MEM, d), mem_norm_w[l:l + 1], bf(w_mem_kv[l]))
        y_mem = _memattn(u3, bf(w_in_l[:, c1:c1 + BR_W]), mkv.reshape(b, N_MEM, 2 * BR_W))
        c2 = c1 + BR_W
        h = _mix(h, u, bf(w_in_l[:, c2:]), b_gate[l:l + 1], y_hg.reshape(t, BR_W), dil_outs,
                 y_mem.reshape(t, BR_W), bf(w_br_hg[l]), bf(w_br_dil[l]), bf(w_br_mem[l]), bf(w_out[l]),
                 mix_post_w[l:l + 1], seq=s)
        (h,) = _ffn(h, ffn2_pre_w[l:l + 1], bf(ffn2_w_gu[l]), bf(ffn2_w_down[l]), ffn2_post_w[l:l + 1])
    return h.reshape(b, s, d)
```

```python
import functools

import jax
import jax.numpy as jnp
import numpy as np
from jax import lax
from jax.experimental import pallas as pl
from jax.experimental.pallas import tpu as pltpu

F32 = jnp.float32
BF16 = jnp.bfloat16

EPS = 1e-6
D_MODEL = 1024
D_FF = 2816
N_MEM = 256
HEADS = 4
DH = 128
LANES = 128
D_SLABS = D_MODEL // LANES
BR_W = HEADS * DH
HG_CHUNK = 64
HG_COLS = 4 * BR_W
DIL_GROUPS = ((128, 1), (512, 4), (2048, 16))
PERM_DILS = tuple(d for _, d in DIL_GROUPS if d > 1)
NK = 128
DIL_GCOLS = 3 * BR_W
N_BRANCH = 3
ATT_SCALE = DH ** -0.5
ALIBI_HEADS = len(DIL_GROUPS) * HEADS

VMEM_LIMIT_BYTES = 56 * 1024 * 1024

FFN_TM = 512
FFN_SUBTILES = 2
FFN_CK = 256
FFN_ZPARTS = 2
HG_TS = 512
HG_GROUP = 256
DIL_TQ = 1024
MEM_TQ = 1024
MEM_PARTS = 2
DIL_PARTS = 4
MIX_TM = 512


def _dot(a, b):
    return jnp.dot(a, b, preferred_element_type=F32)


def _dot_nt(a, b):
    return lax.dot_general(a, b, (((1,), (1,)), ((), ())), preferred_element_type=F32)


def _dot_tn(a, b):
    return lax.dot_general(a, b, (((0,), (0,)), ((), ())), preferred_element_type=F32)


def _rms(x, w):
    ms = jnp.mean(x * x, axis=-1, keepdims=True)
    return x * lax.rsqrt(ms + EPS) * w


def _sigmoid(x):
    return 0.5 * jnp.tanh(0.5 * x) + 0.5


def _const_spec(shape):
    zeros = (0,) * len(shape)
    return pl.BlockSpec(shape, lambda *_: zeros, pipeline_mode=pl.Buffered(1))


def _params(*semantics):
    return pltpu.CompilerParams(dimension_semantics=semantics, vmem_limit_bytes=VMEM_LIMIT_BYTES)


def _ffn_tile(h_ref, prew_ref, wgu_ref, wd_ref, postw_ref, a_scr, row0=0):
    tile = slice(row0, row0 + FFN_TM)
    x = h_ref[tile, :]
    xw = (x * prew_ref[...]).astype(BF16)
    r = lax.rsqrt(jnp.mean(x * x, axis=-1, keepdims=True) + EPS)
    for c in range(D_FF // FFN_CK):
        lo = c * FFN_CK
        g = r * _dot(xw, wgu_ref[:, lo:lo + FFN_CK])
        u = r * _dot(xw, wgu_ref[:, D_FF + lo:D_FF + lo + FFN_CK])
        a_scr[tile, lo:lo + FFN_CK] = (g * _sigmoid(g) * u).astype(BF16)
    outs = []
    part_rows = FFN_TM // FFN_ZPARTS
    for part in range(FFN_ZPARTS):
        rows = slice(row0 + part * part_rows, row0 + (part + 1) * part_rows)
        z = _dot(a_scr[rows, :], wd_ref[...])
        rz = lax.rsqrt(jnp.mean(z * z, axis=-1, keepdims=True) + EPS)
        outs.append(h_ref[rows, :] + (z * rz) * (0.5 * postw_ref[...]))
    return jnp.concatenate(outs, axis=0)


def _ffn_kernel(h_ref, prew_ref, wgu_ref, wd_ref, postw_ref, o_ref, a_scr):
    for sub in range(FFN_SUBTILES):
        row0 = sub * FFN_TM
        o_ref[row0:row0 + FFN_TM, :] = _ffn_tile(h_ref, prew_ref, wgu_ref, wd_ref, postw_ref, a_scr, row0)


def _ffn_prenorm_kernel(h_ref, prew_ref, wgu_ref, wd_ref, postw_ref, nextw_ref, o_ref, un_ref, p4_ref, p16_ref,
                        a_scr, slab_scr, slab4_scr):
    out = _ffn_tile(h_ref, prew_ref, wgu_ref, wd_ref, postw_ref, a_scr)
    o_ref[...] = out
    un = _rms(out, nextw_ref[...])
    un_ref[...] = un.astype(BF16)
    n4 = FFN_TM // 4
    n16 = FFN_TM // 16
    for c in range(D_SLABS):
        lanes = slice(c * LANES, (c + 1) * LANES)
        slab_scr[c] = un[:, lanes]
        for r4 in range(4):
            rows = slab_scr[c, pl.ds(r4, n4, stride=4), :]
            p4_ref[0, r4, :, lanes] = rows.astype(BF16)
            slab4_scr[c, r4 * n4:(r4 + 1) * n4, :] = rows
        for r4 in range(4):
            for q in range(4):
                rows = slab4_scr[c, pl.ds(r4 * n4 + q, n16, stride=4), :]
                p16_ref[0, r4 + 4 * q, :, lanes] = rows.astype(BF16)


def _ffn(h, pre_w, w_gu, w_down, post_w, next_w=None, *, seq=None, dils=()):
    t = h.shape[0]
    tm = FFN_TM if next_w is not None else FFN_SUBTILES * FFN_TM
    row_spec = pl.BlockSpec((tm, D_MODEL), lambda i: (i, 0))
    in_specs = [row_spec, _const_spec((1, D_MODEL)), _const_spec((D_MODEL, 2 * D_FF)),
                _const_spec((D_FF, D_MODEL)), _const_spec((1, D_MODEL))]
    args = [h, pre_w, w_gu, w_down, post_w]
    out_specs = [row_spec]
    out_shape = [jax.ShapeDtypeStruct((t, D_MODEL), F32)]
    scratch = [pltpu.VMEM((tm, D_FF), BF16)]
    kern = _ffn_kernel
    if next_w is not None:
        assert dils == (4, 16)
        tiles_per_seq = seq // FFN_TM
        kern = _ffn_prenorm_kernel
        in_specs.append(_const_spec((1, D_MODEL)))
        args.append(next_w)
        out_specs.append(row_spec)
        out_shape.append(jax.ShapeDtypeStruct((t, D_MODEL), BF16))
        for d in dils:
            out_specs.append(pl.BlockSpec((1, d, FFN_TM // d, D_MODEL),
                                          lambda i: (i // tiles_per_seq, 0, i % tiles_per_seq, 0)))
            out_shape.append(jax.ShapeDtypeStruct((t // seq, d, seq // d, D_MODEL), BF16))
        scratch += [pltpu.VMEM((D_SLABS, FFN_TM, LANES), F32)] * 2
    return pl.pallas_call(
        kern,
        grid=(t // tm,),
        in_specs=in_specs,
        out_specs=out_specs,
        out_shape=out_shape,
        scratch_shapes=scratch,
        compiler_params=_params("parallel"),
        name="ffn",
    )(*args)


def _hgrn_masks():
    i = np.arange(HG_GROUP)[:, None]
    j = np.arange(HG_GROUP)[None, :]
    same = (i // HG_CHUNK) == (j // HG_CHUNK)
    mid = (i // HG_CHUNK) * HG_CHUNK + HG_CHUNK // 2 - 1
    dmat = same * ((j <= i).astype(np.float32) - (j <= mid).astype(np.float32))
    c = np.arange(16)[:, None]
    jt = np.arange(HG_TS)[None, :]
    smat = ((jt // HG_CHUNK) == c) & (jt % HG_CHUNK < HG_CHUNK // 2)
    return jnp.asarray(dmat, BF16), jnp.asarray(smat.astype(np.float32), BF16)


def _hgrn_kernel(u_ref, w_ref, lbl_ref, nw_ref, dmat_ref, smat_ref, o_ref, proj_scr, oint_scr, st_scr):
    @pl.when(pl.program_id(1) == 0)
    def _():
        st_scr[...] = jnp.zeros_like(st_scr)

    u = u_ref[0]
    proj_scr[:, BR_W:2 * BR_W] = _dot(u, w_ref[:, BR_W:2 * BR_W])
    proj_scr[:, 0:BR_W] = _dot(u, w_ref[:, 0:BR_W])
    proj_scr[:, 2 * BR_W:4 * BR_W] = _dot(u, w_ref[:, 2 * BR_W:4 * BR_W])

    lg = lbl_ref[...]
    e = jnp.exp(lg - jnp.max(lg, axis=0, keepdims=True))
    lb = e[0:1] / jnp.sum(e, axis=0, keepdims=True)

    c_ = HG_CHUNK
    fb = 0.5 * (1.0 - lb)
    ft = fb * jnp.tanh(0.5 * proj_scr[:, BR_W:2 * BR_W])
    f = (lb + fb) + ft
    k = fb - ft
    lf = jnp.log(f)
    lf0 = lf.astype(BF16)
    lf1 = (lf - lf0.astype(F32)).astype(BF16)
    group_rows = [slice(g * HG_GROUP, (g + 1) * HG_GROUP) for g in range(HG_TS // HG_GROUP)]
    bd = jnp.concatenate([_dot(dmat_ref[...], lf0[rows]) + _dot(dmat_ref[...], lf1[rows])
                          for rows in group_rows], axis=0)
    bref = _dot(smat_ref[...], lf0) + _dot(smat_ref[...], lf1)
    qr = proj_scr[:, 0:BR_W]
    qe32 = qr * _sigmoid(qr) * jnp.exp(bd)
    ke32 = k * jnp.exp(-bd)
    qe = qe32.astype(BF16)
    ke = ke32.astype(BF16)
    v = proj_scr[:, 2 * BR_W:3 * BR_W].astype(BF16)

    n_chunks = HG_TS // c_
    qb, kd, dec = [], [], []
    for c in range(n_chunks):
        rows = slice(c * c_, (c + 1) * c_)
        tail = bd[(c + 1) * c_ - 1:(c + 1) * c_]
        brc = bref[c:c + 1]
        kd.append((ke32[rows] * jnp.exp(tail)).astype(BF16))
        qb.append((qe32[rows] * jnp.exp(brc)).astype(BF16))
        dec.append(jnp.exp(brc + tail))

    row = lax.broadcasted_iota(jnp.int32, (HG_GROUP, HG_GROUP), 0)
    col = lax.broadcasted_iota(jnp.int32, (HG_GROUP, HG_GROUP), 1)
    causal = (row >= col) & ((row // c_) == (col // c_))
    nw = nw_ref[...]

    head_sl = [slice(h * DH, (h + 1) * DH) for h in range(HEADS)]
    chunk_rows = [slice(c * c_, (c + 1) * c_) for c in range(n_chunks)]
    kv_t = [[_dot_tn(v[chunk_rows[c], sl], kd[c][:, sl]) for sl in head_sl] for c in range(n_chunks)]
    st = [st_scr[h] for h in range(HEADS)]
    st_in = []
    for c in range(n_chunks):
        st_in.append([s.astype(BF16) for s in st])
        st = [st[h] * dec[c][:, head_sl[h]] + kv_t[c][h] for h in range(HEADS)]
    for h in range(HEADS):
        st_scr[h] = st[h]
    for c in range(n_chunks):
        for h, sl in enumerate(head_sl):
            oint_scr[chunk_rows[c], sl] = _dot_nt(qb[c][:, sl], st_in[c][h])

    for h in range(HEADS):
        sl = head_sl[h]
        for g in range(HG_TS // HG_GROUP):
            rows = slice(g * HG_GROUP, (g + 1) * HG_GROUP)
            sc = jnp.where(causal, _dot_nt(qe[rows, sl], ke[rows, sl]), 0.0).astype(BF16)
            o = _dot(sc, v[rows, sl]) + oint_scr[rows, sl]
            y = _rms(o, nw) * _sigmoid(proj_scr[rows, 3 * BR_W + h * DH:3 * BR_W + (h + 1) * DH])
            o_ref[0, rows, sl] = y.astype(BF16)


def _hgrn(u, w_hg, lb_logits, norm_w):
    b, s, _ = u.shape
    dmat, smat = _hgrn_masks()
    return pl.pallas_call(
        _hgrn_kernel,
        grid=(b, s // HG_TS),
        in_specs=[
            pl.BlockSpec((1, HG_TS, D_MODEL), lambda bi, i: (bi, i, 0)),
            _const_spec((D_MODEL, HG_COLS)),
            _const_spec(lb_logits.shape),
            _const_spec((1, DH)),
            _const_spec(dmat.shape),
            _const_spec(smat.shape),
        ],
        out_specs=pl.BlockSpec((1, HG_TS, BR_W), lambda bi, i: (bi, i, 0)),
        out_shape=jax.ShapeDtypeStruct((b, s, BR_W), BF16),
        scratch_shapes=[pltpu.VMEM((HG_TS, HG_COLS), F32), pltpu.VMEM((HG_TS, BR_W), F32),
                        pltpu.VMEM((HEADS, DH, DH), F32)],
        compiler_params=_params("parallel", "arbitrary"),
        name="hgrn",
    )(u, w_hg, lb_logits, norm_w, dmat, smat)


def _dil_bias(dil, slopes):
    qi = np.arange(NK)[:, None]
    kj = np.arange(2 * NK)[None, :]
    delta = NK + qi - kj
    band = (delta >= 0) & (delta <= NK)
    dist = (delta * dil).astype(np.float32)
    rows = [[np.where(valid, -np.float32(sl) * dist, -np.inf) for sl in slopes]
            for valid in (band, band & (kj >= NK))]
    return jnp.asarray(np.array(rows, np.float32))


def _dil_kernel(u_ref, w_ref, bias_ref, o_ref, lse_ref, q_scr, kv_scr, *, nr, tq):
    i = pl.program_id(2)
    rows = nr * tq

    @pl.when(i == 0)
    def _():
        kv_scr[0:NK, :] = jnp.zeros((NK, 2 * BR_W), BF16)

    @pl.when(i > 0)
    def _():
        kv_scr[0:NK, :] = kv_scr[rows:rows + NK, :]

    seq_start = jnp.where(i == 0, 1, 0)
    lane = lax.broadcasted_iota(jnp.int32, (NK, LANES), 1)
    part_rows = rows // DIL_PARTS
    u2d = u_ref[0].reshape(rows, D_MODEL)
    for part in range(DIL_PARTS):
        p0 = part * part_rows
        qkv = _dot(u2d[p0:p0 + part_rows], w_ref[...])
        q_scr[p0:p0 + part_rows, :] = qkv[:, 0:BR_W].astype(BF16)
        kv_scr[NK + p0:NK + p0 + part_rows, :] = qkv[:, BR_W:3 * BR_W].astype(BF16)
        for lo in range(p0, p0 + part_rows, NK):
            rr, j = lo // tq, (lo % tq) // NK
            lse_tile = jnp.zeros((NK, LANES), F32)
            for h in range(HEADS):
                sl = slice(h * DH, (h + 1) * DH)
                q = q_scr[lo:lo + NK, sl]
                kk = kv_scr[lo:lo + 2 * NK, sl]
                vv = kv_scr[lo:lo + 2 * NK, BR_W + h * DH:BR_W + (h + 1) * DH]
                bias = bias_ref[seq_start, h] if j == 0 else bias_ref[0, h]
                s = _dot_nt(q, kk) * ATT_SCALE + bias
                m = jnp.max(s, axis=-1, keepdims=True)
                p = jnp.exp(s - m)
                l = jnp.sum(p, axis=-1, keepdims=True)
                o = _dot(p.astype(BF16), vv)
                o_ref[0, rr, j * NK:(j + 1) * NK, sl] = (o / l).astype(BF16)
                lse_tile = jnp.where(lane == h, m + jnp.log(l), lse_tile)
            lse_ref[0, rr, j * NK:(j + 1) * NK, :] = lse_tile


def _dil_group(up, w_g, dil_bias):
    b, dil, l, _ = up.shape
    tq = min(DIL_TQ, l)
    nr = DIL_TQ // tq
    kern = functools.partial(_dil_kernel, nr=nr, tq=tq)
    return pl.pallas_call(
        kern,
        grid=(b, dil // nr, l // tq),
        in_specs=[
            pl.BlockSpec((1, nr, tq, D_MODEL), lambda bi, r, i: (bi, r, i, 0)),
            _const_spec((D_MODEL, DIL_GCOLS)),
            _const_spec(dil_bias.shape),
        ],
        out_specs=[pl.BlockSpec((1, nr, tq, BR_W), lambda bi, r, i: (bi, r, i, 0)),
                   pl.BlockSpec((1, nr, tq, LANES), lambda bi, r, i: (bi, r, i, 0))],
        out_shape=[jax.ShapeDtypeStruct((b, dil, l, BR_W), BF16),
                   jax.ShapeDtypeStruct((b, dil, l, LANES), F32)],
        scratch_shapes=[pltpu.VMEM((nr * tq, BR_W), BF16), pltpu.VMEM((NK + nr * tq, 2 * BR_W), BF16)],
        compiler_params=_params("parallel", "parallel", "arbitrary"),
        name=f"dil{dil}",
    )(up, w_g, dil_bias)


def _memkv_kernel(m_ref, nw_ref, w_ref, o_ref):
    n = _rms(m_ref[...], nw_ref[...]).astype(BF16)
    o_ref[...] = _dot(n, w_ref[...]).astype(BF16)


def _memkv(mem2d, norm_w, w_kv):
    t = mem2d.shape[0]
    tm = 512
    return pl.pallas_call(
        _memkv_kernel,
        grid=(t // tm,),
        in_specs=[
            pl.BlockSpec((tm, D_MODEL), lambda i: (i, 0)),
            _const_spec((1, D_MODEL)),
            _const_spec((D_MODEL, 2 * BR_W)),
        ],
        out_specs=pl.BlockSpec((tm, 2 * BR_W), lambda i: (i, 0)),
        out_shape=jax.ShapeDtypeStruct((t, 2 * BR_W), BF16),
        compiler_params=_params("parallel"),
        name="memkv",
    )(mem2d, norm_w, w_kv)


def _memattn_kernel(u_ref, w_ref, kv_ref, o_ref):
    part_rows = MEM_TQ // MEM_PARTS
    for part in range(MEM_PARTS):
        rows = slice(part * part_rows, (part + 1) * part_rows)
        mq = _dot(u_ref[0, rows, :], w_ref[...]).astype(BF16)
        for h in range(HEADS):
            sl = slice(h * DH, (h + 1) * DH)
            s = _dot_nt(mq[:, sl], kv_ref[0, :, sl]) * ATT_SCALE
            e = jnp.exp(s - jnp.max(s, axis=-1, keepdims=True))
            p = e * (1.0 / jnp.sum(e, axis=-1, keepdims=True))
            o = _dot(p.astype(BF16), kv_ref[0, :, BR_W + h * DH:BR_W + (h + 1) * DH])
            o_ref[0, rows, sl] = o.astype(BF16)


def _memattn(u, w_mq, mkv):
    b, s, _ = u.shape
    return pl.pallas_call(
        _memattn_kernel,
        grid=(b, s // MEM_TQ),
        in_specs=[
            pl.BlockSpec((1, MEM_TQ, D_MODEL), lambda bi, i: (bi, i, 0)),
            _const_spec((D_MODEL, BR_W)),
            pl.BlockSpec((1, N_MEM, 2 * BR_W), lambda bi, i: (bi, 0, 0)),
        ],
        out_specs=pl.BlockSpec((1, MEM_TQ, BR_W), lambda bi, i: (bi, i, 0)),
        out_shape=jax.ShapeDtypeStruct((b, s, BR_W), BF16),
        compiler_params=_params("parallel", "parallel"),
        name="memattn",
    )(u, w_mq, mkv)


def _mix_kernel(h_ref, u_ref, wg_ref, bg_ref, yhg_ref, o0_ref, l0_ref, o1_ref, l1_ref, o2_ref, l2_ref,
                ymem_ref, whg_ref, wdil_ref, wmem_ref, wout_ref, postw_ref, out_ref,
                o1_scr, o2_scr, otmp_scr, l1_scr, l2_scr, ltmp_scr):
    u = u_ref[...]

    def gate(bidx):
        sl = slice(bidx * D_MODEL, (bidx + 1) * D_MODEL)
        return _sigmoid(_dot(u, wg_ref[:, sl]) + bg_ref[:, sl])

    n4 = MIX_TM // 4
    n16 = MIX_TM // 16
    for r4 in range(4):
        l1_scr[pl.ds(r4, n4, stride=4), :] = l1_ref[0, r4]
        for h in range(HEADS):
            o1_scr[h, pl.ds(r4, n4, stride=4), :] = o1_ref[0, r4, :, h * DH:(h + 1) * DH].astype(F32)
    for r4 in range(4):
        for q in range(4):
            ltmp_scr[pl.ds(r4 * n4 + q, n16, stride=4), :] = l2_ref[0, r4 + 4 * q]
        l2_scr[pl.ds(r4, n4, stride=4), :] = ltmp_scr[r4 * n4:(r4 + 1) * n4, :]
    for h in range(HEADS):
        for r4 in range(4):
            for q in range(4):
                otmp_scr[h, pl.ds(r4 * n4 + q, n16, stride=4), :] = (
                    o2_ref[0, r4 + 4 * q, :, h * DH:(h + 1) * DH].astype(F32))
        for r4 in range(4):
            o2_scr[h, pl.ds(r4, n4, stride=4), :] = otmp_scr[h, r4 * n4:(r4 + 1) * n4, :]

    l0, l1, l2 = l0_ref[0, 0], l1_scr[...], l2_scr[...]
    mx = jnp.maximum(jnp.maximum(l0, l1), l2)
    e0, e1, e2 = jnp.exp(l0 - mx), jnp.exp(l1 - mx), jnp.exp(l2 - mx)
    inv = 1.0 / (e0 + e1 + e2)
    w0, w1, w2 = e0 * inv, e1 * inv, e2 * inv
    heads = []
    for h in range(HEADS):
        sl = slice(h * DH, (h + 1) * DH)
        b0, b1, b2 = (jnp.broadcast_to(w[:, h:h + 1], (MIX_TM, DH)) for w in (w0, w1, w2))
        merged = b0 * o0_ref[0, 0, :, sl].astype(F32) + b1 * o1_scr[h] + b2 * o2_scr[h]
        heads.append(merged.astype(BF16))
    ydil = jnp.concatenate(heads, axis=-1)

    y = gate(0) * _dot(yhg_ref[...], whg_ref[...])
    y = y + gate(1) * _dot(ydil, wdil_ref[...])
    y = y + gate(2) * _dot(ymem_ref[...], wmem_ref[...])
    yb = y.astype(BF16)
    half = MIX_TM // 2
    for part in range(2):
        rows = slice(part * half, (part + 1) * half)
        z = _dot(yb[rows], wout_ref[...])
        out_ref[rows, :] = h_ref[rows, :] + _rms(z, postw_ref[...])


def _mix(h1, u, w_gate, b_gate, y_hg, dil_outs, y_mem, w_br_hg, w_br_dil, w_br_mem, w_out, post_w, *, seq):
    t = h1.shape[0]
    tm = MIX_TM
    tiles_per_seq = seq // tm
    row_d = pl.BlockSpec((tm, D_MODEL), lambda i: (i, 0))
    row_b = pl.BlockSpec((tm, BR_W), lambda i: (i, 0))

    def perm_specs(d):
        index = lambda i: (i // tiles_per_seq, 0, i % tiles_per_seq, 0)
        return [pl.BlockSpec((1, d, tm // d, BR_W), index), pl.BlockSpec((1, d, tm // d, LANES), index)]

    (o0, l0), (o1, l1), (o2, l2) = dil_outs
    d1, d2 = PERM_DILS
    assert (d1, d2) == (4, 16)
    return pl.pallas_call(
        _mix_kernel,
        grid=(t // tm,),
        in_specs=[
            row_d,
            row_d,
            _const_spec((D_MODEL, N_BRANCH * D_MODEL)),
            _const_spec((1, N_BRANCH * D_MODEL)),
            row_b,
            *perm_specs(1), *perm_specs(d1), *perm_specs(d2),
            row_b,
            _const_spec((BR_W, D_MODEL)),
            _const_spec((BR_W, D_MODEL)),
            _const_spec((BR_W, D_MODEL)),
            _const_spec((D_MODEL, D_MODEL)),
            _const_spec((1, D_MODEL)),
        ],
        out_specs=row_d,
        out_shape=jax.ShapeDtypeStruct((t, D_MODEL), F32),
        scratch_shapes=[pltpu.VMEM((HEADS, tm, LANES), F32)] * 3 + [pltpu.VMEM((tm, LANES), F32)] * 3,
        compiler_params=_params("parallel"),
        name="mix",
    )(h1, u, w_gate, b_gate, y_hg, o0, l0, o1, l1, o2, l2, y_mem,
      w_br_hg, w_br_dil, w_br_mem, w_out, post_w)


def _alibi_slopes(group):
    return tuple(2.0 ** (-8.0 * (group * HEADS + h + 1) / ALIBI_HEADS) for h in range(HEADS))


def kernel(x, mem, ffn1_pre_w, ffn1_w_gu, ffn1_w_down, ffn1_post_w, mix_pre_w, w_in, b_gate, hg_lb_logits,
           hg_norm_w, mem_norm_w, w_mem_kv, w_br_hg, w_br_dil, w_br_mem, w_out, mix_post_w, ffn2_pre_w,
           ffn2_w_gu, ffn2_w_down, ffn2_post_w):
    b, s, d = x.shape
    t = b * s
    bf = lambda a: a.astype(BF16)
    depth = ffn1_pre_w.shape[0]
    h = x.reshape(t, d)
    for l in range(depth):
        assert depth == 1
        h, u, *u_perm = _ffn(h, ffn1_pre_w[l:l + 1], bf(ffn1_w_gu[l]), bf(ffn1_w_down[l]), ffn1_post_w[l:l + 1],
                             mix_pre_w[l:l + 1], seq=s, dils=PERM_DILS)
        u3 = u.reshape(b, s, d)
        u_by_dil = dict(zip(PERM_DILS, u_perm))
        u_by_dil[1] = u.reshape(b, 1, s, d)
        w_in_l = w_in[l]
        c0 = HG_COLS
        y_hg = _hgrn(u3, bf(w_in_l[:, :c0]), hg_lb_logits, hg_norm_w[l:l + 1])
        dil_outs = []
        for g, (_, dil) in enumerate(DIL_GROUPS):
            w_g = bf(w_in_l[:, c0 + g * DIL_GCOLS:c0 + (g + 1) * DIL_GCOLS])
            dil_outs.append(_dil_group(u_by_dil[dil], w_g, _dil_bias(dil, _alibi_slopes(g))))
        c1 = c0 + len(DIL_GROUPS) * DIL_GCOLS
        mkv = _memkv(mem.reshape(b * N_MEM, d), mem_norm_w[l:l + 1], bf(w_mem_kv[l]))
        y_mem = _memattn(u3, bf(w_in_l[:, c1:c1 + BR_W]), mkv.reshape(b, N_MEM, 2 * BR_W))
        c2 = c1 + BR_W
        h = _mix(h, u, bf(w_in_l[:, c2:]), b_gate[l:l + 1], y_hg.reshape(t, BR_W), dil_outs,
                 y_mem.reshape(t, BR_W), bf(w_br_hg[l]), bf(w_br_dil[l]), bf(w_br_mem[l]), bf(w_out[l]),
                 mix_post_w[l:l + 1], seq=s)
        (h,) = _ffn(h, ffn2_pre_w[l:l + 1], bf(ffn2_w_gu[l]), bf(ffn2_w_down[l]), ffn2_post_w[l:l + 1])
    return h.reshape(b, s, d)
```

```python
import functools

import jax
import jax.numpy as jnp
import numpy as np
from jax import lax
from jax.experimental import pallas as pl
from jax.experimental.pallas import tpu as pltpu

F32 = jnp.float32
BF16 = jnp.bfloat16

EPS = 1e-6
D_MODEL = 1024
D_FF = 2816
N_MEM = 256
HEADS = 4
DH = 128
LANES = 128
D_SLABS = D_MODEL // LANES
BR_W = HEADS * DH
HG_CHUNK = 64
HG_COLS = 4 * BR_W
DIL_GROUPS = ((128, 1), (512, 4), (2048, 16))
PERM_DILS = tuple(d for _, d in DIL_GROUPS if d > 1)
NK = 128
DIL_GCOLS = 3 * BR_W
N_BRANCH = 3
ATT_SCALE = DH ** -0.5
ALIBI_HEADS = len(DIL_GROUPS) * HEADS

VMEM_LIMIT_BYTES = 58 * 1024 * 1024

FFN_TM = 512
FFN_SUBTILES = 2
FFN_CK = 256
FFN_ZPARTS = 2
HG_TS = 512
HG_GROUP = 256
HG_SEQS = 2
DIL_TQ = 1024
MEM_TQ = 1024
MEM_PARTS = 2
DIL_PARTS = 4
MIX_TM = 512


def _dot(a, b):
    return jnp.dot(a, b, preferred_element_type=F32)


def _dot_nt(a, b):
    return lax.dot_general(a, b, (((1,), (1,)), ((), ())), preferred_element_type=F32)


def _dot_tn(a, b):
    return lax.dot_general(a, b, (((0,), (0,)), ((), ())), preferred_element_type=F32)


def _rms(x, w):
    ms = jnp.mean(x * x, axis=-1, keepdims=True)
    return x * lax.rsqrt(ms + EPS) * w


def _sigmoid(x):
    return 0.5 * jnp.tanh(0.5 * x) + 0.5


def _const_spec(shape):
    zeros = (0,) * len(shape)
    return pl.BlockSpec(shape, lambda *_: zeros, pipeline_mode=pl.Buffered(1))


def _params(*semantics):
    return pltpu.CompilerParams(dimension_semantics=semantics, vmem_limit_bytes=VMEM_LIMIT_BYTES)


def _ffn_tile(h_ref, prew_ref, wgu_ref, wd_ref, postw_ref, a_scr, row0=0):
    x = h_ref[row0:row0 + FFN_TM, :]
    xw = (x * prew_ref[...]).astype(BF16)
    r = lax.rsqrt(jnp.mean(x * x, axis=-1, keepdims=True) + EPS)
    for c in range(D_FF // FFN_CK):
        lo = c * FFN_CK
        g = r * _dot(xw, wgu_ref[:, lo:lo + FFN_CK])
        u = r * _dot(xw, wgu_ref[:, D_FF + lo:D_FF + lo + FFN_CK])
        a_scr[:, lo:lo + FFN_CK] = (g * _sigmoid(g) * u).astype(BF16)
    outs = []
    part_rows = FFN_TM // FFN_ZPARTS
    for part in range(FFN_ZPARTS):
        rows = slice(part * part_rows, (part + 1) * part_rows)
        z = _dot(a_scr[rows, :], wd_ref[...])
        rz = lax.rsqrt(jnp.mean(z * z, axis=-1, keepdims=True) + EPS)
        outs.append(x[rows] + (z * rz) * (0.5 * postw_ref[...]))
    return jnp.concatenate(outs, axis=0)


def _ffn_kernel(h_ref, prew_ref, wgu_ref, wd_ref, postw_ref, o_ref, a_scr):
    for sub in range(FFN_SUBTILES):
        row0 = sub * FFN_TM
        o_ref[row0:row0 + FFN_TM, :] = _ffn_tile(h_ref, prew_ref, wgu_ref, wd_ref, postw_ref, a_scr, row0)


def _ffn_prenorm_kernel(h_ref, prew_ref, wgu_ref, wd_ref, postw_ref, nextw_ref, o_ref, un_ref, p4_ref, p16_ref,
                        a_scr, slab_scr, slab4_scr):
    n4 = FFN_TM // 4
    n16 = FFN_TM // 16
    for sub in range(FFN_SUBTILES):
        row0 = sub * FFN_TM
        out = _ffn_tile(h_ref, prew_ref, wgu_ref, wd_ref, postw_ref, a_scr, row0)
        o_ref[row0:row0 + FFN_TM, :] = out
        un = _rms(out, nextw_ref[...])
        un_ref[row0:row0 + FFN_TM, :] = un.astype(BF16)
        for c in range(D_SLABS):
            lanes = slice(c * LANES, (c + 1) * LANES)
            slab_scr[c] = un[:, lanes]
            for r4 in range(4):
                rows = slab_scr[c, pl.ds(r4, n4, stride=4), :]
                p4_ref[0, r4, sub * n4:(sub + 1) * n4, lanes] = rows.astype(BF16)
                slab4_scr[c, r4 * n4:(r4 + 1) * n4, :] = rows
            for r4 in range(4):
                for q in range(4):
                    rows = slab4_scr[c, pl.ds(r4 * n4 + q, n16, stride=4), :]
                    p16_ref[0, r4 + 4 * q, sub * n16:(sub + 1) * n16, lanes] = rows.astype(BF16)


def _ffn(h, pre_w, w_gu, w_down, post_w, next_w=None, *, seq=None, dils=()):
    t = h.shape[0]
    tm = FFN_SUBTILES * FFN_TM
    row_spec = pl.BlockSpec((tm, D_MODEL), lambda i: (i, 0))
    in_specs = [row_spec, _const_spec((1, D_MODEL)), _const_spec((D_MODEL, 2 * D_FF)),
                _const_spec((D_FF, D_MODEL)), _const_spec((1, D_MODEL))]
    args = [h, pre_w, w_gu, w_down, post_w]
    out_specs = [row_spec]
    out_shape = [jax.ShapeDtypeStruct((t, D_MODEL), F32)]
    scratch = [pltpu.VMEM((FFN_TM, D_FF), BF16)]
    kern = _ffn_kernel
    if next_w is not None:
        assert dils == (4, 16)
        tiles_per_seq = seq // tm
        kern = _ffn_prenorm_kernel
        in_specs.append(_const_spec((1, D_MODEL)))
        args.append(next_w)
        out_specs.append(row_spec)
        out_shape.append(jax.ShapeDtypeStruct((t, D_MODEL), BF16))
        for d in dils:
            out_specs.append(pl.BlockSpec((1, d, tm // d, D_MODEL),
                                          lambda i: (i // tiles_per_seq, 0, i % tiles_per_seq, 0)))
            out_shape.append(jax.ShapeDtypeStruct((t // seq, d, seq // d, D_MODEL), BF16))
        scratch += [pltpu.VMEM((D_SLABS, FFN_TM, LANES), F32)] * 2
    return pl.pallas_call(
        kern,
        grid=(t // tm,),
        in_specs=in_specs,
        out_specs=out_specs,
        out_shape=out_shape,
        scratch_shapes=scratch,
        compiler_params=_params("parallel"),
        name="ffn",
    )(*args)


def _hgrn_masks():
    i = np.arange(HG_GROUP)[:, None]
    j = np.arange(HG_GROUP)[None, :]
    same = (i // HG_CHUNK) == (j // HG_CHUNK)
    mid = (i // HG_CHUNK) * HG_CHUNK + HG_CHUNK // 2 - 1
    dmat = same * ((j <= i).astype(np.float32) - (j <= mid).astype(np.float32))
    c = np.arange(16)[:, None]
    jt = np.arange(HG_TS)[None, :]
    smat = ((jt // HG_CHUNK) == c) & (jt % HG_CHUNK < HG_CHUNK // 2)
    return jnp.asarray(dmat, BF16), jnp.asarray(smat.astype(np.float32), BF16)


def _hgrn_kernel(u_ref, w_ref, lbl_ref, nw_ref, dmat_ref, smat_ref, o_ref, proj_scr, oint_scr, st_scr):
    for s in range(HG_SEQS):
        _hgrn_tile(u_ref.at[pl.ds(s, 1)], w_ref, lbl_ref, nw_ref, dmat_ref, smat_ref, o_ref.at[pl.ds(s, 1)],
                   proj_scr.at[s], oint_scr.at[s], st_scr.at[s])


def _hgrn_tile(u_ref, w_ref, lbl_ref, nw_ref, dmat_ref, smat_ref, o_ref, proj_scr, oint_scr, st_scr):
    @pl.when(pl.program_id(1) == 0)
    def _():
        st_scr[...] = jnp.zeros_like(st_scr)

    u = u_ref[0]
    proj_scr[:, BR_W:2 * BR_W] = _dot(u, w_ref[:, BR_W:2 * BR_W])
    proj_scr[:, 0:BR_W] = _dot(u, w_ref[:, 0:BR_W])
    proj_scr[:, 2 * BR_W:4 * BR_W] = _dot(u, w_ref[:, 2 * BR_W:4 * BR_W])

    lg = lbl_ref[...]
    e = jnp.exp(lg - jnp.max(lg, axis=0, keepdims=True))
    lb = e[0:1] / jnp.sum(e, axis=0, keepdims=True)

    c_ = HG_CHUNK
    fb = 0.5 * (1.0 - lb)
    ft = fb * jnp.tanh(0.5 * proj_scr[:, BR_W:2 * BR_W])
    f = (lb + fb) + ft
    k = fb - ft
    lf = jnp.log(f)
    lf0 = lf.astype(BF16)
    lf1 = (lf - lf0.astype(F32)).astype(BF16)
    group_rows = [slice(g * HG_GROUP, (g + 1) * HG_GROUP) for g in range(HG_TS // HG_GROUP)]
    bd = jnp.concatenate([_dot(dmat_ref[...], lf0[rows]) + _dot(dmat_ref[...], lf1[rows])
                          for rows in group_rows], axis=0)
    bref = _dot(smat_ref[...], lf0) + _dot(smat_ref[...], lf1)
    qr = proj_scr[:, 0:BR_W]
    qe32 = qr * _sigmoid(qr) * jnp.exp(bd)
    ke32 = k * jnp.exp(-bd)
    qe = qe32.astype(BF16)
    ke = ke32.astype(BF16)
    v = proj_scr[:, 2 * BR_W:3 * BR_W].astype(BF16)

    n_chunks = HG_TS // c_
    qb, kd, dec = [], [], []
    for c in range(n_chunks):
        rows = slice(c * c_, (c + 1) * c_)
        tail = bd[(c + 1) * c_ - 1:(c + 1) * c_]
        brc = bref[c:c + 1]
        kd.append((ke32[rows] * jnp.exp(tail)).astype(BF16))
        qb.append((qe32[rows] * jnp.exp(brc)).astype(BF16))
        dec.append(jnp.exp(brc + tail))

    row = lax.broadcasted_iota(jnp.int32, (HG_GROUP, HG_GROUP), 0)
    col = lax.broadcasted_iota(jnp.int32, (HG_GROUP, HG_GROUP), 1)
    causal = (row >= col) & ((row // c_) == (col // c_))
    nw = nw_ref[...]

    head_sl = [slice(h * DH, (h + 1) * DH) for h in range(HEADS)]
    chunk_rows = [slice(c * c_, (c + 1) * c_) for c in range(n_chunks)]
    kv_t = [[_dot_tn(v[chunk_rows[c], sl], kd[c][:, sl]) for sl in head_sl] for c in range(n_chunks)]
    st = [st_scr[h] for h in range(HEADS)]
    st_in = []
    for c in range(n_chunks):
        st_in.append([s.astype(BF16) for s in st])
        st = [st[h] * dec[c][:, head_sl[h]] + kv_t[c][h] for h in range(HEADS)]
    for h in range(HEADS):
        st_scr[h] = st[h]
    for c in range(n_chunks):
        for h, sl in enumerate(head_sl):
            oint_scr[chunk_rows[c], sl] = _dot_nt(qb[c][:, sl], st_in[c][h])

    for h in range(HEADS):
        sl = head_sl[h]
        for g in range(HG_TS // HG_GROUP):
            rows = slice(g * HG_GROUP, (g + 1) * HG_GROUP)
            sc = jnp.where(causal, _dot_nt(qe[rows, sl], ke[rows, sl]), 0.0).astype(BF16)
            o = _dot(sc, v[rows, sl]) + oint_scr[rows, sl]
            y = _rms(o, nw) * _sigmoid(proj_scr[rows, 3 * BR_W + h * DH:3 * BR_W + (h + 1) * DH])
            o_ref[0, rows, sl] = y.astype(BF16)


def _hgrn(u, w_hg, lb_logits, norm_w):
    b, s, _ = u.shape
    dmat, smat = _hgrn_masks()
    return pl.pallas_call(
        _hgrn_kernel,
        grid=(b // HG_SEQS, s // HG_TS),
        in_specs=[
            pl.BlockSpec((HG_SEQS, HG_TS, D_MODEL), lambda bi, i: (bi, i, 0)),
            _const_spec((D_MODEL, HG_COLS)),
            _const_spec(lb_logits.shape),
            _const_spec((1, DH)),
            _const_spec(dmat.shape),
            _const_spec(smat.shape),
        ],
        out_specs=pl.BlockSpec((HG_SEQS, HG_TS, BR_W), lambda bi, i: (bi, i, 0)),
        out_shape=jax.ShapeDtypeStruct((b, s, BR_W), BF16),
        scratch_shapes=[pltpu.VMEM((HG_SEQS, HG_TS, HG_COLS), F32), pltpu.VMEM((HG_SEQS, HG_TS, BR_W), F32),
                        pltpu.VMEM((HG_SEQS, HEADS, DH, DH), F32)],
        compiler_params=_params("parallel", "arbitrary"),
        name="hgrn",
    )(u, w_hg, lb_logits, norm_w, dmat, smat)


def _dil_bias(dil, slopes):
    qi = np.arange(NK)[:, None]
    kj = np.arange(2 * NK)[None, :]
    delta = NK + qi - kj
    band = (delta >= 0) & (delta <= NK)
    dist = (delta * dil).astype(np.float32)
    rows = [[np.where(valid, -np.float32(sl) * dist, -np.inf) for sl in slopes]
            for valid in (band, band & (kj >= NK))]
    return jnp.asarray(np.array(rows, np.float32))


def _dil_kernel(u_ref, w_ref, bias_ref, o_ref, lse_ref, q_scr, kv_scr, *, nr, tq):
    i = pl.program_id(2)
    rows = nr * tq

    @pl.when(i == 0)
    def _():
        kv_scr[0:NK, :] = jnp.zeros((NK, 2 * BR_W), BF16)

    @pl.when(i > 0)
    def _():
        kv_scr[0:NK, :] = kv_scr[rows:rows + NK, :]

    seq_start = jnp.where(i == 0, 1, 0)
    lane = lax.broadcasted_iota(jnp.int32, (NK, LANES), 1)
    part_rows = rows // DIL_PARTS
    u2d = u_ref[0].reshape(rows, D_MODEL)
    for part in range(DIL_PARTS):
        p0 = part * part_rows
        qkv = _dot(u2d[p0:p0 + part_rows], w_ref[...])
        q_scr[p0:p0 + part_rows, :] = qkv[:, 0:BR_W].astype(BF16)
        kv_scr[NK + p0:NK + p0 + part_rows, :] = qkv[:, BR_W:3 * BR_W].astype(BF16)
        for lo in range(p0, p0 + part_rows, NK):
            rr, j = lo // tq, (lo % tq) // NK
            lse_tile = jnp.zeros((NK, LANES), F32)
            for h in range(HEADS):
                sl = slice(h * DH, (h + 1) * DH)
                q = q_scr[lo:lo + NK, sl]
                kk = kv_scr[lo:lo + 2 * NK, sl]
                vv = kv_scr[lo:lo + 2 * NK, BR_W + h * DH:BR_W + (h + 1) * DH]
                bias = bias_ref[seq_start, h] if j == 0 else bias_ref[0, h]
                s = _dot_nt(q, kk) * ATT_SCALE + bias
                m = jnp.max(s, axis=-1, keepdims=True)
                p = jnp.exp(s - m)
                l = jnp.sum(p, axis=-1, keepdims=True)
                o = _dot(p.astype(BF16), vv)
                o_ref[0, rr, j * NK:(j + 1) * NK, sl] = (o / l).astype(BF16)
                lse_tile = jnp.where(lane == h, m + jnp.log(l), lse_tile)
            lse_ref[0, rr, j * NK:(j + 1) * NK, :] = lse_tile


def _dil_group(up, w_g, dil_bias):
    b, dil, l, _ = up.shape
    tq = min(DIL_TQ, l)
    nr = DIL_TQ // tq
    kern = functools.partial(_dil_kernel, nr=nr, tq=tq)
    return pl.pallas_call(
        kern,
        grid=(b, dil // nr, l // tq),
        in_specs=[
            pl.BlockSpec((1, nr, tq, D_MODEL), lambda bi, r, i: (bi, r, i, 0)),
            _const_spec((D_MODEL, DIL_GCOLS)),
            _const_spec(dil_bias.shape),
        ],
        out_specs=[pl.BlockSpec((1, nr, tq, BR_W), lambda bi, r, i: (bi, r, i, 0)),
                   pl.BlockSpec((1, nr, tq, LANES), lambda bi, r, i: (bi, r, i, 0))],
        out_shape=[jax.ShapeDtypeStruct((b, dil, l, BR_W), BF16),
                   jax.ShapeDtypeStruct((b, dil, l, LANES), F32)],
        scratch_shapes=[pltpu.VMEM((nr * tq, BR_W), BF16), pltpu.VMEM((NK + nr * tq, 2 * BR_W), BF16)],
        compiler_params=_params("parallel", "parallel", "arbitrary"),
        name=f"dil{dil}",
    )(up, w_g, dil_bias)


def _memkv_kernel(m_ref, nw_ref, w_ref, o_ref):
    n = _rms(m_ref[...], nw_ref[...]).astype(BF16)
    o_ref[...] = _dot(n, w_ref[...]).astype(BF16)


def _memkv(mem2d, norm_w, w_kv):
    t = mem2d.shape[0]
    tm = 512
    return pl.pallas_call(
        _memkv_kernel,
        grid=(t // tm,),
        in_specs=[
            pl.BlockSpec((tm, D_MODEL), lambda i: (i, 0)),
            _const_spec((1, D_MODEL)),
            _const_spec((D_MODEL, 2 * BR_W)),
        ],
        out_specs=pl.BlockSpec((tm, 2 * BR_W), lambda i: (i, 0)),
        out_shape=jax.ShapeDtypeStruct((t, 2 * BR_W), BF16),
        compiler_params=_params("parallel"),
        name="memkv",
    )(mem2d, norm_w, w_kv)


def _memattn_kernel(u_ref, w_ref, kv_ref, o_ref):
    part_rows = MEM_TQ // MEM_PARTS
    for part in range(MEM_PARTS):
        rows = slice(part * part_rows, (part + 1) * part_rows)
        mq = _dot(u_ref[0, rows, :], w_ref[...]).astype(BF16)
        for h in range(HEADS):
            sl = slice(h * DH, (h + 1) * DH)
            s = _dot_nt(mq[:, sl], kv_ref[0, :, sl]) * ATT_SCALE
            e = jnp.exp(s - jnp.max(s, axis=-1, keepdims=True))
            p = e * (1.0 / jnp.sum(e, axis=-1, keepdims=True))
            o = _dot(p.astype(BF16), kv_ref[0, :, BR_W + h * DH:BR_W + (h + 1) * DH])
            o_ref[0, rows, sl] = o.astype(BF16)


def _memattn(u, w_mq, mkv):
    b, s, _ = u.shape
    return pl.pallas_call(
        _memattn_kernel,
        grid=(b, s // MEM_TQ),
        in_specs=[
            pl.BlockSpec((1, MEM_TQ, D_MODEL), lambda bi, i: (bi, i, 0)),
            _const_spec((D_MODEL, BR_W)),
            pl.BlockSpec((1, N_MEM, 2 * BR_W), lambda bi, i: (bi, 0, 0)),
        ],
        out_specs=pl.BlockSpec((1, MEM_TQ, BR_W), lambda bi, i: (bi, i, 0)),
        out_shape=jax.ShapeDtypeStruct((b, s, BR_W), BF16),
        compiler_params=_params("parallel", "parallel"),
        name="memattn",
    )(u, w_mq, mkv)


def _mix_kernel(h_ref, u_ref, wg_ref, bg_ref, yhg_ref, o0_ref, l0_ref, o1_ref, l1_ref, o2_ref, l2_ref,
                ymem_ref, whg_ref, wdil_ref, wmem_ref, wout_ref, postw_ref, out_ref,
                o1_scr, o2_scr, otmp_scr, l1_scr, l2_scr, ltmp_scr):
    u = u_ref[...]

    def gate(bidx):
        sl = slice(bidx * D_MODEL, (bidx + 1) * D_MODEL)
        return _sigmoid(_dot(u, wg_ref[:, sl]) + bg_ref[:, sl])

    n4 = MIX_TM // 4
    n16 = MIX_TM // 16
    for r4 in range(4):
        l1_scr[pl.ds(r4, n4, stride=4), :] = l1_ref[0, r4]
        for h in range(HEADS):
            o1_scr[h, pl.ds(r4, n4, stride=4), :] = o1_ref[0, r4, :, h * DH:(h + 1) * DH].astype(F32)
    for r4 in range(4):
        for q in range(4):
            ltmp_scr[pl.ds(r4 * n4 + q, n16, stride=4), :] = l2_ref[0, r4 + 4 * q]
        l2_scr[pl.ds(r4, n4, stride=4), :] = ltmp_scr[r4 * n4:(r4 + 1) * n4, :]
    for h in range(HEADS):
        for r4 in range(4):
            for q in range(4):
                otmp_scr[h, pl.ds(r4 * n4 + q, n16, stride=4), :] = (
                    o2_ref[0, r4 + 4 * q, :, h * DH:(h + 1) * DH].astype(F32))
        for r4 in range(4):
            o2_scr[h, pl.ds(r4, n4, stride=4), :] = otmp_scr[h, r4 * n4:(r4 + 1) * n4, :]

    l0, l1, l2 = l0_ref[0, 0], l1_scr[...], l2_scr[...]
    mx = jnp.maximum(jnp.maximum(l0, l1), l2)
    e0, e1, e2 = jnp.exp(l0 - mx), jnp.exp(l1 - mx), jnp.exp(l2 - mx)
    inv = 1.0 / (e0 + e1 + e2)
    w0, w1, w2 = e0 * inv, e1 * inv, e2 * inv
    heads = []
    for h in range(HEADS):
        sl = slice(h * DH, (h + 1) * DH)
        b0, b1, b2 = (jnp.broadcast_to(w[:, h:h + 1], (MIX_TM, DH)) for w in (w0, w1, w2))
        merged = b0 * o0_ref[0, 0, :, sl].astype(F32) + b1 * o1_scr[h] + b2 * o2_scr[h]
        heads.append(merged.astype(BF16))
    ydil = jnp.concatenate(heads, axis=-1)

    y = gate(0) * _dot(yhg_ref[...], whg_ref[...])
    y = y + gate(1) * _dot(ydil, wdil_ref[...])
    y = y + gate(2) * _dot(ymem_ref[...], wmem_ref[...])
    yb = y.astype(BF16)
    half = MIX_TM // 2
    for part in range(2):
        rows = slice(part * half, (part + 1) * half)
        z = _dot(yb[rows], wout_ref[...])
        out_ref[rows, :] = h_ref[rows, :] + _rms(z, postw_ref[...])


def _mix(h1, u, w_gate, b_gate, y_hg, dil_outs, y_mem, w_br_hg, w_br_dil, w_br_mem, w_out, post_w, *, seq):
    t = h1.shape[0]
    tm = MIX_TM
    tiles_per_seq = seq // tm
    row_d = pl.BlockSpec((tm, D_MODEL), lambda i: (i, 0))
    row_b = pl.BlockSpec((tm, BR_W), lambda i: (i, 0))

    def perm_specs(d):
        index = lambda i: (i // tiles_per_seq, 0, i % tiles_per_seq, 0)
        return [pl.BlockSpec((1, d, tm // d, BR_W), index), pl.BlockSpec((1, d, tm // d, LANES), index)]

    (o0, l0), (o1, l1), (o2, l2) = dil_outs
    d1, d2 = PERM_DILS
    assert (d1, d2) == (4, 16)
    return pl.pallas_call(
        _mix_kernel,
        grid=(t // tm,),
        in_specs=[
            row_d,
            row_d,
            _const_spec((D_MODEL, N_BRANCH * D_MODEL)),
            _const_spec((1, N_BRANCH * D_MODEL)),
            row_b,
            *perm_specs(1), *perm_specs(d1), *perm_specs(d2),
            row_b,
            _const_spec((BR_W, D_MODEL)),
            _const_spec((BR_W, D_MODEL)),
            _const_spec((BR_W, D_MODEL)),
            _const_spec((D_MODEL, D_MODEL)),
            _const_spec((1, D_MODEL)),
        ],
        out_specs=row_d,
        out_shape=jax.ShapeDtypeStruct((t, D_MODEL), F32),
        scratch_shapes=[pltpu.VMEM((HEADS, tm, LANES), F32)] * 3 + [pltpu.VMEM((tm, LANES), F32)] * 3,
        compiler_params=_params("parallel"),
        name="mix",
    )(h1, u, w_gate, b_gate, y_hg, o0, l0, o1, l1, o2, l2, y_mem,
      w_br_hg, w_br_dil, w_br_mem, w_out, post_w)


def _alibi_slopes(group):
    return tuple(2.0 ** (-8.0 * (group * HEADS + h + 1) / ALIBI_HEADS) for h in range(HEADS))


def kernel(x, mem, ffn1_pre_w, ffn1_w_gu, ffn1_w_down, ffn1_post_w, mix_pre_w, w_in, b_gate, hg_lb_logits,
           hg_norm_w, mem_norm_w, w_mem_kv, w_br_hg, w_br_dil, w_br_mem, w_out, mix_post_w, ffn2_pre_w,
           ffn2_w_gu, ffn2_w_down, ffn2_post_w):
    b, s, d = x.shape
    t = b * s
    bf = lambda a: a.astype(BF16)
    depth = ffn1_pre_w.shape[0]
    h = x.reshape(t, d)
    for l in range(depth):
        assert depth == 1
        h, u, *u_perm = _ffn(h, ffn1_pre_w[l:l + 1], bf(ffn1_w_gu[l]), bf(ffn1_w_down[l]), ffn1_post_w[l:l + 1],
                             mix_pre_w[l:l + 1], seq=s, dils=PERM_DILS)
        u3 = u.reshape(b, s, d)
        u_by_dil = dict(zip(PERM_DILS, u_perm))
        u_by_dil[1] = u.reshape(b, 1, s, d)
        w_in_l = w_in[l]
        c0 = HG_COLS
        y_hg = _hgrn(u3, bf(w_in_l[:, :c0]), hg_lb_logits, hg_norm_w[l:l + 1])
        dil_outs = []
        for g, (_, dil) in enumerate(DIL_GROUPS):
            w_g = bf(w_in_l[:, c0 + g * DIL_GCOLS:c0 + (g + 1) * DIL_GCOLS])
            dil_outs.append(_dil_group(u_by_dil[dil], w_g, _dil_bias(dil, _alibi_slopes(g))))
        c1 = c0 + len(DIL_GROUPS) * DIL_GCOLS
        mkv = _memkv(mem.reshape(b * N_MEM, d), mem_norm_w[l:l + 1], bf(w_mem_kv[l]))
        y_mem = _memattn(u3, bf(w_in_l[:, c1:c1 + BR_W]), mkv.reshape(b, N_MEM, 2 * BR_W))
        c2 = c1 + BR_W
        h = _mix(h, u, bf(w_in_l[:, c2:]), b_gate[l:l + 1], y_hg.reshape(t, BR_W), dil_outs,
                 y_mem.reshape(t, BR_W), bf(w_br_hg[l]), bf(w_br_dil[l]), bf(w_br_mem[l]), bf(w_out[l]),
                 mix_post_w[l:l + 1], seq=s)
        (h,) = _ffn(h, ffn2_pre_w[l:l + 1], bf(ffn2_w_gu[l]), bf(ffn2_w_down[l]), ffn2_post_w[l:l + 1])
    return h.reshape(b, s, d)
```

```python
import functools

import jax
import jax.numpy as jnp
import numpy as np
from jax import lax
from jax.experimental import pallas as pl
from jax.experimental.pallas import tpu as pltpu

F32 = jnp.float32
BF16 = jnp.bfloat16

EPS = 1e-6
D_MODEL = 1024
D_FF = 2816
N_MEM = 256
HEADS = 4
DH = 128
LANES = 128
D_SLABS = D_MODEL // LANES
BR_W = HEADS * DH
HG_CHUNK = 64
HG_COLS = 4 * BR_W
DIL_GROUPS = ((128, 1), (512, 4), (2048, 16))
PERM_DILS = tuple(d for _, d in DIL_GROUPS if d > 1)
NK = 128
DIL_GCOLS = 3 * BR_W
N_BRANCH = 3
ATT_SCALE = DH ** -0.5
ALIBI_HEADS = len(DIL_GROUPS) * HEADS

VMEM_LIMIT_BYTES = 58 * 1024 * 1024

FFN_TM = 512
FFN_SUBTILES = 2
FFN_CK = 256
FFN_ZPARTS = 2
HG_TS = 512
HG_GROUP = 256
HG_SEQS = 1
DIL_TQ = 2048
DIL_PARTS = 8
MEM_TQ = 2048
MEM_PARTS = 4
MIX_TM = 512


def _dot(a, b):
    return jnp.dot(a, b, preferred_element_type=F32)


def _dot_nt(a, b):
    return lax.dot_general(a, b, (((1,), (1,)), ((), ())), preferred_element_type=F32)


def _dot_tn(a, b):
    return lax.dot_general(a, b, (((0,), (0,)), ((), ())), preferred_element_type=F32)


def _rms(x, w):
    ms = jnp.mean(x * x, axis=-1, keepdims=True)
    return x * lax.rsqrt(ms + EPS) * w


def _sigmoid(x):
    return 0.5 * jnp.tanh(0.5 * x) + 0.5


def _const_spec(shape):
    zeros = (0,) * len(shape)
    return pl.BlockSpec(shape, lambda *_: zeros, pipeline_mode=pl.Buffered(1))


def _params(*semantics):
    return pltpu.CompilerParams(dimension_semantics=semantics, vmem_limit_bytes=VMEM_LIMIT_BYTES)


def _ffn_tile(h_ref, prew_ref, wgu_ref, wd_ref, postw_ref, a_scr, row0=0):
    x = h_ref[row0:row0 + FFN_TM, :]
    xw = (x * prew_ref[...]).astype(BF16)
    r = lax.rsqrt(jnp.mean(x * x, axis=-1, keepdims=True) + EPS)
    for c in range(D_FF // FFN_CK):
        lo = c * FFN_CK
        g = r * _dot(xw, wgu_ref[:, lo:lo + FFN_CK])
        u = r * _dot(xw, wgu_ref[:, D_FF + lo:D_FF + lo + FFN_CK])
        a_scr[:, lo:lo + FFN_CK] = (g * _sigmoid(g) * u).astype(BF16)
    outs = []
    part_rows = FFN_TM // FFN_ZPARTS
    for part in range(FFN_ZPARTS):
        rows = slice(part * part_rows, (part + 1) * part_rows)
        z = _dot(a_scr[rows, :], wd_ref[...])
        rz = lax.rsqrt(jnp.mean(z * z, axis=-1, keepdims=True) + EPS)
        outs.append(x[rows] + (z * rz) * (0.5 * postw_ref[...]))
    return jnp.concatenate(outs, axis=0)


def _ffn_kernel(h_ref, prew_ref, wgu_ref, wd_ref, postw_ref, o_ref, a_scr):
    for sub in range(FFN_SUBTILES):
        row0 = sub * FFN_TM
        o_ref[row0:row0 + FFN_TM, :] = _ffn_tile(h_ref, prew_ref, wgu_ref, wd_ref, postw_ref, a_scr, row0)


def _ffn_prenorm_kernel(h_ref, prew_ref, wgu_ref, wd_ref, postw_ref, nextw_ref, o_ref, un_ref, p4_ref, p16_ref,
                        a_scr, slab_scr, slab4_scr):
    n4 = FFN_TM // 4
    n16 = FFN_TM // 16
    for sub in range(FFN_SUBTILES):
        row0 = sub * FFN_TM
        out = _ffn_tile(h_ref, prew_ref, wgu_ref, wd_ref, postw_ref, a_scr, row0)
        o_ref[row0:row0 + FFN_TM, :] = out
        un = _rms(out, nextw_ref[...])
        un_ref[row0:row0 + FFN_TM, :] = un.astype(BF16)
        for c in range(D_SLABS):
            lanes = slice(c * LANES, (c + 1) * LANES)
            slab_scr[c] = un[:, lanes]
            for r4 in range(4):
                rows = slab_scr[c, pl.ds(r4, n4, stride=4), :]
                p4_ref[0, r4, sub * n4:(sub + 1) * n4, lanes] = rows.astype(BF16)
                slab4_scr[c, r4 * n4:(r4 + 1) * n4, :] = rows
            for r4 in range(4):
                for q in range(4):
                    rows = slab4_scr[c, pl.ds(r4 * n4 + q, n16, stride=4), :]
                    p16_ref[0, r4 + 4 * q, sub * n16:(sub + 1) * n16, lanes] = rows.astype(BF16)


def _ffn(h, pre_w, w_gu, w_down, post_w, next_w=None, *, seq=None, dils=()):
    t = h.shape[0]
    tm = FFN_SUBTILES * FFN_TM
    row_spec = pl.BlockSpec((tm, D_MODEL), lambda i: (i, 0))
    in_specs = [row_spec, _const_spec((1, D_MODEL)), _const_spec((D_MODEL, 2 * D_FF)),
                _const_spec((D_FF, D_MODEL)), _const_spec((1, D_MODEL))]
    args = [h, pre_w, w_gu, w_down, post_w]
    out_specs = [row_spec]
    out_shape = [jax.ShapeDtypeStruct((t, D_MODEL), F32)]
    scratch = [pltpu.VMEM((FFN_TM, D_FF), BF16)]
    kern = _ffn_kernel
    if next_w is not None:
        assert dils == (4, 16)
        tiles_per_seq = seq // tm
        kern = _ffn_prenorm_kernel
        in_specs.append(_const_spec((1, D_MODEL)))
        args.append(next_w)
        out_specs.append(row_spec)
        out_shape.append(jax.ShapeDtypeStruct((t, D_MODEL), BF16))
        for d in dils:
            out_specs.append(pl.BlockSpec((1, d, tm // d, D_MODEL),
                                          lambda i: (i // tiles_per_seq, 0, i % tiles_per_seq, 0)))
            out_shape.append(jax.ShapeDtypeStruct((t // seq, d, seq // d, D_MODEL), BF16))
        scratch += [pltpu.VMEM((D_SLABS, FFN_TM, LANES), F32)] * 2
    return pl.pallas_call(
        kern,
        grid=(t // tm,),
        in_specs=in_specs,
        out_specs=out_specs,
        out_shape=out_shape,
        scratch_shapes=scratch,
        compiler_params=_params("parallel"),
        name="ffn",
    )(*args)


def _hgrn_masks():
    i = np.arange(HG_GROUP)[:, None]
    j = np.arange(HG_GROUP)[None, :]
    same = (i // HG_CHUNK) == (j // HG_CHUNK)
    mid = (i // HG_CHUNK) * HG_CHUNK + HG_CHUNK // 2 - 1
    dmat = same * ((j <= i).astype(np.float32) - (j <= mid).astype(np.float32))
    c = np.arange(16)[:, None]
    jt = np.arange(HG_TS)[None, :]
    smat = ((jt // HG_CHUNK) == c) & (jt % HG_CHUNK < HG_CHUNK // 2)
    return jnp.asarray(dmat, BF16), jnp.asarray(smat.astype(np.float32), BF16)


def _hgrn_kernel(u_ref, w_ref, lbl_ref, nw_ref, dmat_ref, smat_ref, o_ref, proj_scr, oint_scr, st_scr):
    for s in range(HG_SEQS):
        _hgrn_tile(u_ref.at[pl.ds(s, 1)], w_ref, lbl_ref, nw_ref, dmat_ref, smat_ref, o_ref.at[pl.ds(s, 1)],
                   proj_scr.at[s], oint_scr.at[s], st_scr.at[s])


def _hgrn_tile(u_ref, w_ref, lbl_ref, nw_ref, dmat_ref, smat_ref, o_ref, proj_scr, oint_scr, st_scr):
    @pl.when(pl.program_id(1) == 0)
    def _():
        st_scr[...] = jnp.zeros_like(st_scr)

    u = u_ref[0]
    proj_scr[:, BR_W:2 * BR_W] = _dot(u, w_ref[:, BR_W:2 * BR_W])
    proj_scr[:, 0:BR_W] = _dot(u, w_ref[:, 0:BR_W])
    proj_scr[:, 2 * BR_W:4 * BR_W] = _dot(u, w_ref[:, 2 * BR_W:4 * BR_W])

    lg = lbl_ref[...]
    e = jnp.exp(lg - jnp.max(lg, axis=0, keepdims=True))
    lb = e[0:1] / jnp.sum(e, axis=0, keepdims=True)

    c_ = HG_CHUNK
    fb = 0.5 * (1.0 - lb)
    ft = fb * jnp.tanh(0.5 * proj_scr[:, BR_W:2 * BR_W])
    f = (lb + fb) + ft
    k = fb - ft
    lf = jnp.log(f)
    lf0 = lf.astype(BF16)
    lf1 = (lf - lf0.astype(F32)).astype(BF16)
    group_rows = [slice(g * HG_GROUP, (g + 1) * HG_GROUP) for g in range(HG_TS // HG_GROUP)]
    bd = jnp.concatenate([_dot(dmat_ref[...], lf0[rows]) + _dot(dmat_ref[...], lf1[rows])
                          for rows in group_rows], axis=0)
    bref = _dot(smat_ref[...], lf0) + _dot(smat_ref[...], lf1)
    qr = proj_scr[:, 0:BR_W]
    qe32 = qr * _sigmoid(qr) * jnp.exp(bd)
    ke32 = k * jnp.exp(-bd)
    qe = qe32.astype(BF16)
    ke = ke32.astype(BF16)
    v = proj_scr[:, 2 * BR_W:3 * BR_W].astype(BF16)

    n_chunks = HG_TS // c_
    qb, kd, dec = [], [], []
    for c in range(n_chunks):
        rows = slice(c * c_, (c + 1) * c_)
        tail = bd[(c + 1) * c_ - 1:(c + 1) * c_]
        brc = bref[c:c + 1]
        kd.append((ke32[rows] * jnp.exp(tail)).astype(BF16))
        qb.append((qe32[rows] * jnp.exp(brc)).astype(BF16))
        dec.append(jnp.exp(brc + tail))

    row = lax.broadcasted_iota(jnp.int32, (HG_GROUP, HG_GROUP), 0)
    col = lax.broadcasted_iota(jnp.int32, (HG_GROUP, HG_GROUP), 1)
    causal = (row >= col) & ((row // c_) == (col // c_))
    nw = nw_ref[...]

    head_sl = [slice(h * DH, (h + 1) * DH) for h in range(HEADS)]
    chunk_rows = [slice(c * c_, (c + 1) * c_) for c in range(n_chunks)]
    kv_t = [[_dot_tn(v[chunk_rows[c], sl], kd[c][:, sl]) for sl in head_sl] for c in range(n_chunks)]
    st = [st_scr[h] for h in range(HEADS)]
    st_in = []
    for c in range(n_chunks):
        st_in.append([s.astype(BF16) for s in st])
        st = [st[h] * dec[c][:, head_sl[h]] + kv_t[c][h] for h in range(HEADS)]
    for h in range(HEADS):
        st_scr[h] = st[h]
    for c in range(n_chunks):
        for h, sl in enumerate(head_sl):
            oint_scr[chunk_rows[c], sl] = _dot_nt(qb[c][:, sl], st_in[c][h])

    for h in range(HEADS):
        sl = head_sl[h]
        for g in range(HG_TS // HG_GROUP):
            rows = slice(g * HG_GROUP, (g + 1) * HG_GROUP)
            sc = jnp.where(causal, _dot_nt(qe[rows, sl], ke[rows, sl]), 0.0).astype(BF16)
            o = _dot(sc, v[rows, sl]) + oint_scr[rows, sl]
            y = _rms(o, nw) * _sigmoid(proj_scr[rows, 3 * BR_W + h * DH:3 * BR_W + (h + 1) * DH])
            o_ref[0, rows, sl] = y.astype(BF16)


def _hgrn(u, w_hg, lb_logits, norm_w):
    b, s, _ = u.shape
    dmat, smat = _hgrn_masks()
    return pl.pallas_call(
        _hgrn_kernel,
        grid=(b // HG_SEQS, s // HG_TS),
        in_specs=[
            pl.BlockSpec((HG_SEQS, HG_TS, D_MODEL), lambda bi, i: (bi, i, 0)),
            _const_spec((D_MODEL, HG_COLS)),
            _const_spec(lb_logits.shape),
            _const_spec((1, DH)),
            _const_spec(dmat.shape),
            _const_spec(smat.shape),
        ],
        out_specs=pl.BlockSpec((HG_SEQS, HG_TS, BR_W), lambda bi, i: (bi, i, 0)),
        out_shape=jax.ShapeDtypeStruct((b, s, BR_W), BF16),
        scratch_shapes=[pltpu.VMEM((HG_SEQS, HG_TS, HG_COLS), F32), pltpu.VMEM((HG_SEQS, HG_TS, BR_W), F32),
                        pltpu.VMEM((HG_SEQS, HEADS, DH, DH), F32)],
        compiler_params=_params("parallel", "arbitrary"),
        name="hgrn",
    )(u, w_hg, lb_logits, norm_w, dmat, smat)


def _dil_bias(dil, slopes):
    qi = np.arange(NK)[:, None]
    kj = np.arange(2 * NK)[None, :]
    delta = NK + qi - kj
    band = (delta >= 0) & (delta <= NK)
    dist = (delta * dil).astype(np.float32)
    rows = [[np.where(valid, -np.float32(sl) * dist, -np.inf) for sl in slopes]
            for valid in (band, band & (kj >= NK))]
    return jnp.asarray(np.array(rows, np.float32))


def _dil_kernel(u_ref, w_ref, bias_ref, o_ref, lse_ref, q_scr, kv_scr, *, nr, tq):
    i = pl.program_id(2)
    rows = nr * tq

    @pl.when(i == 0)
    def _():
        kv_scr[0:NK, :] = jnp.zeros((NK, 2 * BR_W), BF16)

    @pl.when(i > 0)
    def _():
        kv_scr[0:NK, :] = kv_scr[rows:rows + NK, :]

    seq_start = jnp.where(i == 0, 1, 0)
    lane = lax.broadcasted_iota(jnp.int32, (NK, LANES), 1)
    part_rows = rows // DIL_PARTS
    u2d = u_ref[0].reshape(rows, D_MODEL)
    for part in range(DIL_PARTS):
        p0 = part * part_rows
        qkv = _dot(u2d[p0:p0 + part_rows], w_ref[...])
        q_scr[p0:p0 + part_rows, :] = qkv[:, 0:BR_W].astype(BF16)
        kv_scr[NK + p0:NK + p0 + part_rows, :] = qkv[:, BR_W:3 * BR_W].astype(BF16)
        for lo in range(p0, p0 + part_rows, NK):
            rr, j = lo // tq, (lo % tq) // NK
            lse_tile = jnp.zeros((NK, LANES), F32)
            for h in range(HEADS):
                sl = slice(h * DH, (h + 1) * DH)
                q = q_scr[lo:lo + NK, sl]
                kk = kv_scr[lo:lo + 2 * NK, sl]
                vv = kv_scr[lo:lo + 2 * NK, BR_W + h * DH:BR_W + (h + 1) * DH]
                bias = bias_ref[seq_start, h] if j == 0 else bias_ref[0, h]
                s = _dot_nt(q, kk) * ATT_SCALE + bias
                m = jnp.max(s, axis=-1, keepdims=True)
                p = jnp.exp(s - m)
                l = jnp.sum(p, axis=-1, keepdims=True)
                o = _dot(p.astype(BF16), vv)
                o_ref[0, rr, j * NK:(j + 1) * NK, sl] = (o / l).astype(BF16)
                lse_tile = jnp.where(lane == h, m + jnp.log(l), lse_tile)
            lse_ref[0, rr, j * NK:(j + 1) * NK, :] = lse_tile


def _dil_group(up, w_g, dil_bias):
    b, dil, l, _ = up.shape
    tq = min(DIL_TQ, l)
    nr = DIL_TQ // tq
    kern = functools.partial(_dil_kernel, nr=nr, tq=tq)
    return pl.pallas_call(
        kern,
        grid=(b, dil // nr, l // tq),
        in_specs=[
            pl.BlockSpec((1, nr, tq, D_MODEL), lambda bi, r, i: (bi, r, i, 0)),
            _const_spec((D_MODEL, DIL_GCOLS)),
            _const_spec(dil_bias.shape),
        ],
        out_specs=[pl.BlockSpec((1, nr, tq, BR_W), lambda bi, r, i: (bi, r, i, 0)),
                   pl.BlockSpec((1, nr, tq, LANES), lambda bi, r, i: (bi, r, i, 0))],
        out_shape=[jax.ShapeDtypeStruct((b, dil, l, BR_W), BF16),
                   jax.ShapeDtypeStruct((b, dil, l, LANES), F32)],
        scratch_shapes=[pltpu.VMEM((nr * tq, BR_W), BF16), pltpu.VMEM((NK + nr * tq, 2 * BR_W), BF16)],
        compiler_params=_params("parallel", "parallel", "arbitrary"),
        name=f"dil{dil}",
    )(up, w_g, dil_bias)


def _memkv_kernel(m_ref, nw_ref, w_ref, o_ref):
    n = _rms(m_ref[...], nw_ref[...]).astype(BF16)
    o_ref[...] = _dot(n, w_ref[...]).astype(BF16)


def _memkv(mem2d, norm_w, w_kv):
    t = mem2d.shape[0]
    tm = 512
    return pl.pallas_call(
        _memkv_kernel,
        grid=(t // tm,),
        in_specs=[
            pl.BlockSpec((tm, D_MODEL), lambda i: (i, 0)),
            _const_spec((1, D_MODEL)),
            _const_spec((D_MODEL, 2 * BR_W)),
        ],
        out_specs=pl.BlockSpec((tm, 2 * BR_W), lambda i: (i, 0)),
        out_shape=jax.ShapeDtypeStruct((t, 2 * BR_W), BF16),
        compiler_params=_params("parallel"),
        name="memkv",
    )(mem2d, norm_w, w_kv)


def _memattn_kernel(u_ref, w_ref, kv_ref, o_ref):
    part_rows = MEM_TQ // MEM_PARTS
    for part in range(MEM_PARTS):
        rows = slice(part * part_rows, (part + 1) * part_rows)
        mq = _dot(u_ref[0, rows, :], w_ref[...]).astype(BF16)
        for h in range(HEADS):
            sl = slice(h * DH, (h + 1) * DH)
            s = _dot_nt(mq[:, sl], kv_ref[0, :, sl]) * ATT_SCALE
            e = jnp.exp(s - jnp.max(s, axis=-1, keepdims=True))
            p = e * (1.0 / jnp.sum(e, axis=-1, keepdims=True))
            o = _dot(p.astype(BF16), kv_ref[0, :, BR_W + h * DH:BR_W + (h + 1) * DH])
            o_ref[0, rows, sl] = o.astype(BF16)


def _memattn(u, w_mq, mkv):
    b, s, _ = u.shape
    return pl.pallas_call(
        _memattn_kernel,
        grid=(b, s // MEM_TQ),
        in_specs=[
            pl.BlockSpec((1, MEM_TQ, D_MODEL), lambda bi, i: (bi, i, 0)),
            _const_spec((D_MODEL, BR_W)),
            pl.BlockSpec((1, N_MEM, 2 * BR_W), lambda bi, i: (bi, 0, 0)),
        ],
        out_specs=pl.BlockSpec((1, MEM_TQ, BR_W), lambda bi, i: (bi, i, 0)),
        out_shape=jax.ShapeDtypeStruct((b, s, BR_W), BF16),
        compiler_params=_params("parallel", "parallel"),
        name="memattn",
    )(u, w_mq, mkv)


def _mix_kernel(h_ref, u_ref, wg_ref, bg_ref, yhg_ref, o0_ref, l0_ref, o1_ref, l1_ref, o2_ref, l2_ref,
                ymem_ref, whg_ref, wdil_ref, wmem_ref, wout_ref, postw_ref, out_ref,
                o1_scr, o2_scr, otmp_scr, l1_scr, l2_scr, ltmp_scr):
    u = u_ref[...]

    def gate(bidx):
        sl = slice(bidx * D_MODEL, (bidx + 1) * D_MODEL)
        return _sigmoid(_dot(u, wg_ref[:, sl]) + bg_ref[:, sl])

    n4 = MIX_TM // 4
    n16 = MIX_TM // 16
    for r4 in range(4):
        l1_scr[pl.ds(r4, n4, stride=4), :] = l1_ref[0, r4]
        for h in range(HEADS):
            o1_scr[h, pl.ds(r4, n4, stride=4), :] = o1_ref[0, r4, :, h * DH:(h + 1) * DH].astype(F32)
    for r4 in range(4):
        for q in range(4):
            ltmp_scr[pl.ds(r4 * n4 + q, n16, stride=4), :] = l2_ref[0, r4 + 4 * q]
        l2_scr[pl.ds(r4, n4, stride=4), :] = ltmp_scr[r4 * n4:(r4 + 1) * n4, :]
    for h in range(HEADS):
        for r4 in range(4):
            for q in range(4):
                otmp_scr[h, pl.ds(r4 * n4 + q, n16, stride=4), :] = (
                    o2_ref[0, r4 + 4 * q, :, h * DH:(h + 1) * DH].astype(F32))
        for r4 in range(4):
            o2_scr[h, pl.ds(r4, n4, stride=4), :] = otmp_scr[h, r4 * n4:(r4 + 1) * n4, :]

    l0, l1, l2 = l0_ref[0, 0], l1_scr[...], l2_scr[...]
    mx = jnp.maximum(jnp.maximum(l0, l1), l2)
    e0, e1, e2 = jnp.exp(l0 - mx), jnp.exp(l1 - mx), jnp.exp(l2 - mx)
    inv = 1.0 / (e0 + e1 + e2)
    w0, w1, w2 = e0 * inv, e1 * inv, e2 * inv
    heads = []
    for h in range(HEADS):
        sl = slice(h * DH, (h + 1) * DH)
        b0, b1, b2 = (jnp.broadcast_to(w[:, h:h + 1], (MIX_TM, DH)) for w in (w0, w1, w2))
        merged = b0 * o0_ref[0, 0, :, sl].astype(F32) + b1 * o1_scr[h] + b2 * o2_scr[h]
        heads.append(merged.astype(BF16))
    ydil = jnp.concatenate(heads, axis=-1)

    y = gate(0) * _dot(yhg_ref[...], whg_ref[...])
    y = y + gate(1) * _dot(ydil, wdil_ref[...])
    y = y + gate(2) * _dot(ymem_ref[...], wmem_ref[...])
    yb = y.astype(BF16)
    half = MIX_TM // 2
    for part in range(2):
        rows = slice(part * half, (part + 1) * half)
        z = _dot(yb[rows], wout_ref[...])
        out_ref[rows, :] = h_ref[rows, :] + _rms(z, postw_ref[...])


def _mix(h1, u, w_gate, b_gate, y_hg, dil_outs, y_mem, w_br_hg, w_br_dil, w_br_mem, w_out, post_w, *, seq):
    t = h1.shape[0]
    tm = MIX_TM
    tiles_per_seq = seq // tm
    row_d = pl.BlockSpec((tm, D_MODEL), lambda i: (i, 0))
    row_b = pl.BlockSpec((tm, BR_W), lambda i: (i, 0))

    def perm_specs(d):
        index = lambda i: (i // tiles_per_seq, 0, i % tiles_per_seq, 0)
        return [pl.BlockSpec((1, d, tm // d, BR_W), index), pl.BlockSpec((1, d, tm // d, LANES), index)]

    (o0, l0), (o1, l1), (o2, l2) = dil_outs
    d1, d2 = PERM_DILS
    assert (d1, d2) == (4, 16)
    return pl.pallas_call(
        _mix_kernel,
        grid=(t // tm,),
        in_specs=[
            row_d,
            row_d,
            _const_spec((D_MODEL, N_BRANCH * D_MODEL)),
            _const_spec((1, N_BRANCH * D_MODEL)),
            row_b,
            *perm_specs(1), *perm_specs(d1), *perm_specs(d2),
            row_b,
            _const_spec((BR_W, D_MODEL)),
            _const_spec((BR_W, D_MODEL)),
            _const_spec((BR_W, D_MODEL)),
            _const_spec((D_MODEL, D_MODEL)),
            _const_spec((1, D_MODEL)),
        ],
        out_specs=row_d,
        out_shape=jax.ShapeDtypeStruct((t, D_MODEL), F32),
        scratch_shapes=[pltpu.VMEM((HEADS, tm, LANES), F32)] * 3 + [pltpu.VMEM((tm, LANES), F32)] * 3,
        compiler_params=_params("parallel"),
        name="mix",
    )(h1, u, w_gate, b_gate, y_hg, o0, l0, o1, l1, o2, l2, y_mem,
      w_br_hg, w_br_dil, w_br_mem, w_out, post_w)


def _alibi_slopes(group):
    return tuple(2.0 ** (-8.0 * (group * HEADS + h + 1) / ALIBI_HEADS) for h in range(HEADS))


def kernel(x, mem, ffn1_pre_w, ffn1_w_gu, ffn1_w_down, ffn1_post_w, mix_pre_w, w_in, b_gate, hg_lb_logits,
           hg_norm_w, mem_norm_w, w_mem_kv, w_br_hg, w_br_dil, w_br_mem, w_out, mix_post_w, ffn2_pre_w,
           ffn2_w_gu, ffn2_w_down, ffn2_post_w):
    b, s, d = x.shape
    t = b * s
    bf = lambda a: a.astype(BF16)
    depth = ffn1_pre_w.shape[0]
    h = x.reshape(t, d)
    for l in range(depth):
        assert depth == 1
        h, u, *u_perm = _ffn(h, ffn1_pre_w[l:l + 1], bf(ffn1_w_gu[l]), bf(ffn1_w_down[l]), ffn1_post_w[l:l + 1],
                             mix_pre_w[l:l + 1], seq=s, dils=PERM_DILS)
        u3 = u.reshape(b, s, d)
        u_by_dil = dict(zip(PERM_DILS, u_perm))
        u_by_dil[1] = u.reshape(b, 1, s, d)
        w_in_l = w_in[l]
        c0 = HG_COLS
        y_hg = _hgrn(u3, bf(w_in_l[:, :c0]), hg_lb_logits, hg_norm_w[l:l + 1])
        dil_outs = []
        for g, (_, dil) in enumerate(DIL_GROUPS):
            w_g = bf(w_in_l[:, c0 + g * DIL_GCOLS:c0 + (g + 1) * DIL_GCOLS])
            dil_outs.append(_dil_group(u_by_dil[dil], w_g, _dil_bias(dil, _alibi_slopes(g))))
        c1 = c0 + len(DIL_GROUPS) * DIL_GCOLS
        mkv = _memkv(mem.reshape(b * N_MEM, d), mem_norm_w[l:l + 1], bf(w_mem_kv[l]))
        y_mem = _memattn(u3, bf(w_in_l[:, c1:c1 + BR_W]), mkv.reshape(b, N_MEM, 2 * BR_W))
        c2 = c1 + BR_W
        h = _mix(h, u, bf(w_in_l[:, c2:]), b_gate[l:l + 1], y_hg.reshape(t, BR_W), dil_outs,
                 y_mem.reshape(t, BR_W), bf(w_br_hg[l]), bf(w_br_dil[l]), bf(w_br_mem[l]), bf(w_out[l]),
                 mix_post_w[l:l + 1], seq=s)
        (h,) = _ffn(h, ffn2_pre_w[l:l + 1], bf(ffn2_w_gu[l]), bf(ffn2_w_down[l]), ffn2_post_w[l:l + 1])
    return h.reshape(b, s, d)
```

```python
import functools

import jax
import jax.numpy as jnp
import numpy as np
from jax import lax
from jax.experimental import pallas as pl
from jax.experimental.pallas import tpu as pltpu

F32 = jnp.float32
BF16 = jnp.bfloat16

EPS = 1e-6
D_MODEL = 1024
D_FF = 2816
N_MEM = 256
HEADS = 4
DH = 128
LANES = 128
D_SLABS = D_MODEL // LANES
BR_W = HEADS * DH
HG_CHUNK = 64
HG_COLS = 4 * BR_W
DIL_GROUPS = ((128, 1), (512, 4), (2048, 16))
PERM_DILS = tuple(d for _, d in DIL_GROUPS if d > 1)
NK = 128
DIL_GCOLS = 3 * BR_W
N_BRANCH = 3
ATT_SCALE = DH ** -0.5
ALIBI_HEADS = len(DIL_GROUPS) * HEADS

VMEM_LIMIT_BYTES = 58 * 1024 * 1024

FFN_TM = 512
FFN_SUBTILES = 2
FFN_CK = 256
FFN_ZPARTS = 2
HG_TS = 1024
HG_GROUP = 256
HG_SEQS = 1
DIL_TQ = 2048
DIL_PARTS = 8
MEM_TQ = 2048
MEM_PARTS = 4
MIX_TM = 512


def _dot(a, b):
    return jnp.dot(a, b, preferred_element_type=F32)


def _dot_nt(a, b):
    return lax.dot_general(a, b, (((1,), (1,)), ((), ())), preferred_element_type=F32)


def _dot_tn(a, b):
    return lax.dot_general(a, b, (((0,), (0,)), ((), ())), preferred_element_type=F32)


def _rms(x, w):
    ms = jnp.mean(x * x, axis=-1, keepdims=True)
    return x * lax.rsqrt(ms + EPS) * w


def _sigmoid(x):
    return 0.5 * jnp.tanh(0.5 * x) + 0.5


def _const_spec(shape):
    zeros = (0,) * len(shape)
    return pl.BlockSpec(shape, lambda *_: zeros, pipeline_mode=pl.Buffered(1))


def _params(*semantics):
    return pltpu.CompilerParams(dimension_semantics=semantics, vmem_limit_bytes=VMEM_LIMIT_BYTES)


def _ffn_tile(h_ref, prew_ref, wgu_ref, wd_ref, postw_ref, a_scr, row0=0):
    x = h_ref[row0:row0 + FFN_TM, :]
    xw = (x * prew_ref[...]).astype(BF16)
    r = lax.rsqrt(jnp.mean(x * x, axis=-1, keepdims=True) + EPS)
    for c in range(D_FF // FFN_CK):
        lo = c * FFN_CK
        g = r * _dot(xw, wgu_ref[:, lo:lo + FFN_CK])
        u = r * _dot(xw, wgu_ref[:, D_FF + lo:D_FF + lo + FFN_CK])
        a_scr[:, lo:lo + FFN_CK] = (g * _sigmoid(g) * u).astype(BF16)
    outs = []
    part_rows = FFN_TM // FFN_ZPARTS
    for part in range(FFN_ZPARTS):
        rows = slice(part * part_rows, (part + 1) * part_rows)
        z = _dot(a_scr[rows, :], wd_ref[...])
        rz = lax.rsqrt(jnp.mean(z * z, axis=-1, keepdims=True) + EPS)
        outs.append(x[rows] + (z * rz) * (0.5 * postw_ref[...]))
    return jnp.concatenate(outs, axis=0)


def _ffn_kernel(h_ref, prew_ref, wgu_ref, wd_ref, postw_ref, o_ref, a_scr):
    for sub in range(FFN_SUBTILES):
        row0 = sub * FFN_TM
        o_ref[row0:row0 + FFN_TM, :] = _ffn_tile(h_ref, prew_ref, wgu_ref, wd_ref, postw_ref, a_scr, row0)


def _ffn_prenorm_kernel(h_ref, prew_ref, wgu_ref, wd_ref, postw_ref, nextw_ref, o_ref, un_ref, p4_ref, p16_ref,
                        a_scr, slab_scr, slab4_scr):
    n4 = FFN_TM // 4
    n16 = FFN_TM // 16
    for sub in range(FFN_SUBTILES):
        row0 = sub * FFN_TM
        out = _ffn_tile(h_ref, prew_ref, wgu_ref, wd_ref, postw_ref, a_scr, row0)
        o_ref[row0:row0 + FFN_TM, :] = out
        un = _rms(out, nextw_ref[...])
        un_ref[row0:row0 + FFN_TM, :] = un.astype(BF16)
        for c in range(D_SLABS):
            lanes = slice(c * LANES, (c + 1) * LANES)
            slab_scr[c] = un[:, lanes]
            for r4 in range(4):
                rows = slab_scr[c, pl.ds(r4, n4, stride=4), :]
                p4_ref[0, r4, sub * n4:(sub + 1) * n4, lanes] = rows.astype(BF16)
                slab4_scr[c, r4 * n4:(r4 + 1) * n4, :] = rows
            for r4 in range(4):
                for q in range(4):
                    rows = slab4_scr[c, pl.ds(r4 * n4 + q, n16, stride=4), :]
                    p16_ref[0, r4 + 4 * q, sub * n16:(sub + 1) * n16, lanes] = rows.astype(BF16)


def _ffn(h, pre_w, w_gu, w_down, post_w, next_w=None, *, seq=None, dils=()):
    t = h.shape[0]
    tm = FFN_SUBTILES * FFN_TM
    row_spec = pl.BlockSpec((tm, D_MODEL), lambda i: (i, 0))
    in_specs = [row_spec, _const_spec((1, D_MODEL)), _const_spec((D_MODEL, 2 * D_FF)),
                _const_spec((D_FF, D_MODEL)), _const_spec((1, D_MODEL))]
    args = [h, pre_w, w_gu, w_down, post_w]
    out_specs = [row_spec]
    out_shape = [jax.ShapeDtypeStruct((t, D_MODEL), F32)]
    scratch = [pltpu.VMEM((FFN_TM, D_FF), BF16)]
    kern = _ffn_kernel
    if next_w is not None:
        assert dils == (4, 16)
        tiles_per_seq = seq // tm
        kern = _ffn_prenorm_kernel
        in_specs.append(_const_spec((1, D_MODEL)))
        args.append(next_w)
        out_specs.append(row_spec)
        out_shape.append(jax.ShapeDtypeStruct((t, D_MODEL), BF16))
        for d in dils:
            out_specs.append(pl.BlockSpec((1, d, tm // d, D_MODEL),
                                          lambda i: (i // tiles_per_seq, 0, i % tiles_per_seq, 0)))
            out_shape.append(jax.ShapeDtypeStruct((t // seq, d, seq // d, D_MODEL), BF16))
        scratch += [pltpu.VMEM((D_SLABS, FFN_TM, LANES), F32)] * 2
    return pl.pallas_call(
        kern,
        grid=(t // tm,),
        in_specs=in_specs,
        out_specs=out_specs,
        out_shape=out_shape,
        scratch_shapes=scratch,
        compiler_params=_params("parallel"),
        name="ffn",
    )(*args)


def _hgrn_masks():
    i = np.arange(HG_GROUP)[:, None]
    j = np.arange(HG_GROUP)[None, :]
    same = (i // HG_CHUNK) == (j // HG_CHUNK)
    mid = (i // HG_CHUNK) * HG_CHUNK + HG_CHUNK // 2 - 1
    dmat = same * ((j <= i).astype(np.float32) - (j <= mid).astype(np.float32))
    c = np.arange(16)[:, None]
    jt = np.arange(HG_TS)[None, :]
    smat = ((jt // HG_CHUNK) == c) & (jt % HG_CHUNK < HG_CHUNK // 2)
    return jnp.asarray(dmat, BF16), jnp.asarray(smat.astype(np.float32), BF16)


def _hgrn_kernel(u_ref, w_ref, lbl_ref, nw_ref, dmat_ref, smat_ref, o_ref, proj_scr, oint_scr, st_scr):
    for s in range(HG_SEQS):
        _hgrn_tile(u_ref.at[pl.ds(s, 1)], w_ref, lbl_ref, nw_ref, dmat_ref, smat_ref, o_ref.at[pl.ds(s, 1)],
                   proj_scr.at[s], oint_scr.at[s], st_scr.at[s])


def _hgrn_tile(u_ref, w_ref, lbl_ref, nw_ref, dmat_ref, smat_ref, o_ref, proj_scr, oint_scr, st_scr):
    @pl.when(pl.program_id(1) == 0)
    def _():
        st_scr[...] = jnp.zeros_like(st_scr)

    u = u_ref[0]
    proj_scr[:, BR_W:2 * BR_W] = _dot(u, w_ref[:, BR_W:2 * BR_W])
    proj_scr[:, 0:BR_W] = _dot(u, w_ref[:, 0:BR_W])
    proj_scr[:, 2 * BR_W:4 * BR_W] = _dot(u, w_ref[:, 2 * BR_W:4 * BR_W])

    lg = lbl_ref[...]
    e = jnp.exp(lg - jnp.max(lg, axis=0, keepdims=True))
    lb = e[0:1] / jnp.sum(e, axis=0, keepdims=True)

    c_ = HG_CHUNK
    fb = 0.5 * (1.0 - lb)
    ft = fb * jnp.tanh(0.5 * proj_scr[:, BR_W:2 * BR_W])
    f = (lb + fb) + ft
    k = fb - ft
    lf = jnp.log(f)
    lf0 = lf.astype(BF16)
    lf1 = (lf - lf0.astype(F32)).astype(BF16)
    group_rows = [slice(g * HG_GROUP, (g + 1) * HG_GROUP) for g in range(HG_TS // HG_GROUP)]
    bd = jnp.concatenate([_dot(dmat_ref[...], lf0[rows]) + _dot(dmat_ref[...], lf1[rows])
                          for rows in group_rows], axis=0)
    bref = _dot(smat_ref[...], lf0) + _dot(smat_ref[...], lf1)
    qr = proj_scr[:, 0:BR_W]
    qe32 = qr * _sigmoid(qr) * jnp.exp(bd)
    ke32 = k * jnp.exp(-bd)
    qe = qe32.astype(BF16)
    ke = ke32.astype(BF16)
    v = proj_scr[:, 2 * BR_W:3 * BR_W].astype(BF16)

    n_chunks = HG_TS // c_
    qb, kd, dec = [], [], []
    for c in range(n_chunks):
        rows = slice(c * c_, (c + 1) * c_)
        tail = bd[(c + 1) * c_ - 1:(c + 1) * c_]
        brc = bref[c:c + 1]
        kd.append((ke32[rows] * jnp.exp(tail)).astype(BF16))
        qb.append((qe32[rows] * jnp.exp(brc)).astype(BF16))
        dec.append(jnp.exp(brc + tail))

    row = lax.broadcasted_iota(jnp.int32, (HG_GROUP, HG_GROUP), 0)
    col = lax.broadcasted_iota(jnp.int32, (HG_GROUP, HG_GROUP), 1)
    causal = (row >= col) & ((row // c_) == (col // c_))
    nw = nw_ref[...]

    head_sl = [slice(h * DH, (h + 1) * DH) for h in range(HEADS)]
    chunk_rows = [slice(c * c_, (c + 1) * c_) for c in range(n_chunks)]
    kv_t = [[_dot_tn(v[chunk_rows[c], sl], kd[c][:, sl]) for sl in head_sl] for c in range(n_chunks)]
    st = [st_scr[h] for h in range(HEADS)]
    st_in = []
    for c in range(n_chunks):
        st_in.append([s.astype(BF16) for s in st])
        st = [st[h] * dec[c][:, head_sl[h]] + kv_t[c][h] for h in range(HEADS)]
    for h in range(HEADS):
        st_scr[h] = st[h]
    for c in range(n_chunks):
        for h, sl in enumerate(head_sl):
            oint_scr[chunk_rows[c], sl] = _dot_nt(qb[c][:, sl], st_in[c][h])

    for h in range(HEADS):
        sl = head_sl[h]
        for g in range(HG_TS // HG_GROUP):
            rows = slice(g * HG_GROUP, (g + 1) * HG_GROUP)
            sc = jnp.where(causal, _dot_nt(qe[rows, sl], ke[rows, sl]), 0.0).astype(BF16)
            o = _dot(sc, v[rows, sl]) + oint_scr[rows, sl]
            y = _rms(o, nw) * _sigmoid(proj_scr[rows, 3 * BR_W + h * DH:3 * BR_W + (h + 1) * DH])
            o_ref[0, rows, sl] = y.astype(BF16)


def _hgrn(u, w_hg, lb_logits, norm_w):
    b, s, _ = u.shape
    dmat, smat = _hgrn_masks()
    return pl.pallas_call(
        _hgrn_kernel,
        grid=(b // HG_SEQS, s // HG_TS),
        in_specs=[
            pl.BlockSpec((HG_SEQS, HG_TS, D_MODEL), lambda bi, i: (bi, i, 0)),
            _const_spec((D_MODEL, HG_COLS)),
            _const_spec(lb_logits.shape),
            _const_spec((1, DH)),
            _const_spec(dmat.shape),
            _const_spec(smat.shape),
        ],
        out_specs=pl.BlockSpec((HG_SEQS, HG_TS, BR_W), lambda bi, i: (bi, i, 0)),
        out_shape=jax.ShapeDtypeStruct((b, s, BR_W), BF16),
        scratch_shapes=[pltpu.VMEM((HG_SEQS, HG_TS, HG_COLS), F32), pltpu.VMEM((HG_SEQS, HG_TS, BR_W), F32),
                        pltpu.VMEM((HG_SEQS, HEADS, DH, DH), F32)],
        compiler_params=_params("parallel", "arbitrary"),
        name="hgrn",
    )(u, w_hg, lb_logits, norm_w, dmat, smat)


def _dil_bias(dil, slopes):
    qi = np.arange(NK)[:, None]
    kj = np.arange(2 * NK)[None, :]
    delta = NK + qi - kj
    band = (delta >= 0) & (delta <= NK)
    dist = (delta * dil).astype(np.float32)
    rows = [[np.where(valid, -np.float32(sl) * dist, -np.inf) for sl in slopes]
            for valid in (band, band & (kj >= NK))]
    return jnp.asarray(np.array(rows, np.float32))


def _dil_kernel(u_ref, w_ref, bias_ref, o_ref, lse_ref, q_scr, kv_scr, *, nr, tq):
    i = pl.program_id(2)
    rows = nr * tq

    @pl.when(i == 0)
    def _():
        kv_scr[0:NK, :] = jnp.zeros((NK, 2 * BR_W), BF16)

    @pl.when(i > 0)
    def _():
        kv_scr[0:NK, :] = kv_scr[rows:rows + NK, :]

    seq_start = jnp.where(i == 0, 1, 0)
    lane = lax.broadcasted_iota(jnp.int32, (NK, LANES), 1)
    part_rows = rows // DIL_PARTS
    u2d = u_ref[0].reshape(rows, D_MODEL)
    for part in range(DIL_PARTS):
        p0 = part * part_rows
        qkv = _dot(u2d[p0:p0 + part_rows], w_ref[...])
        q_scr[p0:p0 + part_rows, :] = qkv[:, 0:BR_W].astype(BF16)
        kv_scr[NK + p0:NK + p0 + part_rows, :] = qkv[:, BR_W:3 * BR_W].astype(BF16)
        for lo in range(p0, p0 + part_rows, NK):
            rr, j = lo // tq, (lo % tq) // NK
            lse_tile = jnp.zeros((NK, LANES), F32)
            for h in range(HEADS):
                sl = slice(h * DH, (h + 1) * DH)
                q = q_scr[lo:lo + NK, sl]
                kk = kv_scr[lo:lo + 2 * NK, sl]
                vv = kv_scr[lo:lo + 2 * NK, BR_W + h * DH:BR_W + (h + 1) * DH]
                bias = bias_ref[seq_start, h] if j == 0 else bias_ref[0, h]
                s = _dot_nt(q, kk) * ATT_SCALE + bias
                m = jnp.max(s, axis=-1, keepdims=True)
                p = jnp.exp(s - m)
                l = jnp.sum(p, axis=-1, keepdims=True)
                o = _dot(p.astype(BF16), vv)
                o_ref[0, rr, j * NK:(j + 1) * NK, sl] = (o / l).astype(BF16)
                lse_tile = jnp.where(lane == h, m + jnp.log(l), lse_tile)
            lse_ref[0, rr, j * NK:(j + 1) * NK, :] = lse_tile


def _dil_group(up, w_g, dil_bias):
    b, dil, l, _ = up.shape
    tq = min(DIL_TQ, l)
    nr = DIL_TQ // tq
    kern = functools.partial(_dil_kernel, nr=nr, tq=tq)
    return pl.pallas_call(
        kern,
        grid=(b, dil // nr, l // tq),
        in_specs=[
            pl.BlockSpec((1, nr, tq, D_MODEL), lambda bi, r, i: (bi, r, i, 0)),
            _const_spec((D_MODEL, DIL_GCOLS)),
            _const_spec(dil_bias.shape),
        ],
        out_specs=[pl.BlockSpec((1, nr, tq, BR_W), lambda bi, r, i: (bi, r, i, 0)),
                   pl.BlockSpec((1, nr, tq, LANES), lambda bi, r, i: (bi, r, i, 0))],
        out_shape=[jax.ShapeDtypeStruct((b, dil, l, BR_W), BF16),
                   jax.ShapeDtypeStruct((b, dil, l, LANES), F32)],
        scratch_shapes=[pltpu.VMEM((nr * tq, BR_W), BF16), pltpu.VMEM((NK + nr * tq, 2 * BR_W), BF16)],
        compiler_params=_params("parallel", "parallel", "arbitrary"),
        name=f"dil{dil}",
    )(up, w_g, dil_bias)


def _memkv_kernel(m_ref, nw_ref, w_ref, o_ref):
    n = _rms(m_ref[...], nw_ref[...]).astype(BF16)
    o_ref[...] = _dot(n, w_ref[...]).astype(BF16)


def _memkv(mem2d, norm_w, w_kv):
    t = mem2d.shape[0]
    tm = 512
    return pl.pallas_call(
        _memkv_kernel,
        grid=(t // tm,),
        in_specs=[
            pl.BlockSpec((tm, D_MODEL), lambda i: (i, 0)),
            _const_spec((1, D_MODEL)),
            _const_spec((D_MODEL, 2 * BR_W)),
        ],
        out_specs=pl.BlockSpec((tm, 2 * BR_W), lambda i: (i, 0)),
        out_shape=jax.ShapeDtypeStruct((t, 2 * BR_W), BF16),
        compiler_params=_params("parallel"),
        name="memkv",
    )(mem2d, norm_w, w_kv)


def _memattn_kernel(u_ref, w_ref, kv_ref, o_ref):
    part_rows = MEM_TQ // MEM_PARTS
    for part in range(MEM_PARTS):
        rows = slice(part * part_rows, (part + 1) * part_rows)
        mq = _dot(u_ref[0, rows, :], w_ref[...]).astype(BF16)
        for h in range(HEADS):
            sl = slice(h * DH, (h + 1) * DH)
            s = _dot_nt(mq[:, sl], kv_ref[0, :, sl]) * ATT_SCALE
            e = jnp.exp(s - jnp.max(s, axis=-1, keepdims=True))
            p = e * (1.0 / jnp.sum(e, axis=-1, keepdims=True))
            o = _dot(p.astype(BF16), kv_ref[0, :, BR_W + h * DH:BR_W + (h + 1) * DH])
            o_ref[0, rows, sl] = o.astype(BF16)


def _memattn(u, w_mq, mkv):
    b, s, _ = u.shape
    return pl.pallas_call(
        _memattn_kernel,
        grid=(b, s // MEM_TQ),
        in_specs=[
            pl.BlockSpec((1, MEM_TQ, D_MODEL), lambda bi, i: (bi, i, 0)),
            _const_spec((D_MODEL, BR_W)),
            pl.BlockSpec((1, N_MEM, 2 * BR_W), lambda bi, i: (bi, 0, 0)),
        ],
        out_specs=pl.BlockSpec((1, MEM_TQ, BR_W), lambda bi, i: (bi, i, 0)),
        out_shape=jax.ShapeDtypeStruct((b, s, BR_W), BF16),
        compiler_params=_params("parallel", "parallel"),
        name="memattn",
    )(u, w_mq, mkv)


def _mix_kernel(h_ref, u_ref, wg_ref, bg_ref, yhg_ref, o0_ref, l0_ref, o1_ref, l1_ref, o2_ref, l2_ref,
                ymem_ref, whg_ref, wdil_ref, wmem_ref, wout_ref, postw_ref, out_ref,
                o1_scr, o2_scr, otmp_scr, l1_scr, l2_scr, ltmp_scr):
    u = u_ref[...]

    def gate(bidx):
        sl = slice(bidx * D_MODEL, (bidx + 1) * D_MODEL)
        return _sigmoid(_dot(u, wg_ref[:, sl]) + bg_ref[:, sl])

    n4 = MIX_TM // 4
    n16 = MIX_TM // 16
    for r4 in range(4):
        l1_scr[pl.ds(r4, n4, stride=4), :] = l1_ref[0, r4]
        for h in range(HEADS):
            o1_scr[h, pl.ds(r4, n4, stride=4), :] = o1_ref[0, r4, :, h * DH:(h + 1) * DH].astype(F32)
    for r4 in range(4):
        for q in range(4):
            ltmp_scr[pl.ds(r4 * n4 + q, n16, stride=4), :] = l2_ref[0, r4 + 4 * q]
        l2_scr[pl.ds(r4, n4, stride=4), :] = ltmp_scr[r4 * n4:(r4 + 1) * n4, :]
    for h in range(HEADS):
        for r4 in range(4):
            for q in range(4):
                otmp_scr[h, pl.ds(r4 * n4 + q, n16, stride=4), :] = (
                    o2_ref[0, r4 + 4 * q, :, h * DH:(h + 1) * DH].astype(F32))
        for r4 in range(4):
            o2_scr[h, pl.ds(r4, n4, stride=4), :] = otmp_scr[h, r4 * n4:(r4 + 1) * n4, :]

    l0, l1, l2 = l0_ref[0, 0], l1_scr[...], l2_scr[...]
    mx = jnp.maximum(jnp.maximum(l0, l1), l2)
    e0, e1, e2 = jnp.exp(l0 - mx), jnp.exp(l1 - mx), jnp.exp(l2 - mx)
    inv = 1.0 / (e0 + e1 + e2)
    w0, w1, w2 = e0 * inv, e1 * inv, e2 * inv
    heads = []
    for h in range(HEADS):
        sl = slice(h * DH, (h + 1) * DH)
        b0, b1, b2 = (jnp.broadcast_to(w[:, h:h + 1], (MIX_TM, DH)) for w in (w0, w1, w2))
        merged = b0 * o0_ref[0, 0, :, sl].astype(F32) + b1 * o1_scr[h] + b2 * o2_scr[h]
        heads.append(merged.astype(BF16))
    ydil = jnp.concatenate(heads, axis=-1)

    y = gate(0) * _dot(yhg_ref[...], whg_ref[...])
    y = y + gate(1) * _dot(ydil, wdil_ref[...])
    y = y + gate(2) * _dot(ymem_ref[...], wmem_ref[...])
    yb = y.astype(BF16)
    half = MIX_TM // 2
    for part in range(2):
        rows = slice(part * half, (part + 1) * half)
        z = _dot(yb[rows], wout_ref[...])
        out_ref[rows, :] = h_ref[rows, :] + _rms(z, postw_ref[...])


def _mix(h1, u, w_gate, b_gate, y_hg, dil_outs, y_mem, w_br_hg, w_br_dil, w_br_mem, w_out, post_w, *, seq):
    t = h1.shape[0]
    tm = MIX_TM
    tiles_per_seq = seq // tm
    row_d = pl.BlockSpec((tm, D_MODEL), lambda i: (i, 0))
    row_b = pl.BlockSpec((tm, BR_W), lambda i: (i, 0))

    def perm_specs(d):
        index = lambda i: (i // tiles_per_seq, 0, i % tiles_per_seq, 0)
        return [pl.BlockSpec((1, d, tm // d, BR_W), index), pl.BlockSpec((1, d, tm // d, LANES), index)]

    (o0, l0), (o1, l1), (o2, l2) = dil_outs
    d1, d2 = PERM_DILS
    assert (d1, d2) == (4, 16)
    return pl.pallas_call(
        _mix_kernel,
        grid=(t // tm,),
        in_specs=[
            row_d,
            row_d,
            _const_spec((D_MODEL, N_BRANCH * D_MODEL)),
            _const_spec((1, N_BRANCH * D_MODEL)),
            row_b,
            *perm_specs(1), *perm_specs(d1), *perm_specs(d2),
            row_b,
            _const_spec((BR_W, D_MODEL)),
            _const_spec((BR_W, D_MODEL)),
            _const_spec((BR_W, D_MODEL)),
            _const_spec((D_MODEL, D_MODEL)),
            _const_spec((1, D_MODEL)),
        ],
        out_specs=row_d,
        out_shape=jax.ShapeDtypeStruct((t, D_MODEL), F32),
        scratch_shapes=[pltpu.VMEM((HEADS, tm, LANES), F32)] * 3 + [pltpu.VMEM((tm, LANES), F32)] * 3,
        compiler_params=_params("parallel"),
        name="mix",
    )(h1, u, w_gate, b_gate, y_hg, o0, l0, o1, l1, o2, l2, y_mem,
      w_br_hg, w_br_dil, w_br_mem, w_out, post_w)


def _alibi_slopes(group):
    return tuple(2.0 ** (-8.0 * (group * HEADS + h + 1) / ALIBI_HEADS) for h in range(HEADS))


def kernel(x, mem, ffn1_pre_w, ffn1_w_gu, ffn1_w_down, ffn1_post_w, mix_pre_w, w_in, b_gate, hg_lb_logits,
           hg_norm_w, mem_norm_w, w_mem_kv, w_br_hg, w_br_dil, w_br_mem, w_out, mix_post_w, ffn2_pre_w,
           ffn2_w_gu, ffn2_w_down, ffn2_post_w):
    b, s, d = x.shape
    t = b * s
    bf = lambda a: a.astype(BF16)
    depth = ffn1_pre_w.shape[0]
    h = x.reshape(t, d)
    for l in range(depth):
        assert depth == 1
        h, u, *u_perm = _ffn(h, ffn1_pre_w[l:l + 1], bf(ffn1_w_gu[l]), bf(ffn1_w_down[l]), ffn1_post_w[l:l + 1],
                             mix_pre_w[l:l + 1], seq=s, dils=PERM_DILS)
        u3 = u.reshape(b, s, d)
        u_by_dil = dict(zip(PERM_DILS, u_perm))
        u_by_dil[1] = u.reshape(b, 1, s, d)
        w_in_l = w_in[l]
        c0 = HG_COLS
        y_hg = _hgrn(u3, bf(w_in_l[:, :c0]), hg_lb_logits, hg_norm_w[l:l + 1])
        dil_outs = []
        for g, (_, dil) in enumerate(DIL_GROUPS):
            w_g = bf(w_in_l[:, c0 + g * DIL_GCOLS:c0 + (g + 1) * DIL_GCOLS])
            dil_outs.append(_dil_group(u_by_dil[dil], w_g, _dil_bias(dil, _alibi_slopes(g))))
        c1 = c0 + len(DIL_GROUPS) * DIL_GCOLS
        mkv = _memkv(mem.reshape(b * N_MEM, d), mem_norm_w[l:l + 1], bf(w_mem_kv[l]))
        y_mem = _memattn(u3, bf(w_in_l[:, c1:c1 + BR_W]), mkv.reshape(b, N_MEM, 2 * BR_W))
        c2 = c1 + BR_W
        h = _mix(h, u, bf(w_in_l[:, c2:]), b_gate[l:l + 1], y_hg.reshape(t, BR_W), dil_outs,
                 y_mem.reshape(t, BR_W), bf(w_br_hg[l]), bf(w_br_dil[l]), bf(w_br_mem[l]), bf(w_out[l]),
                 mix_post_w[l:l + 1], seq=s)
        (h,) = _ffn(h, ffn2_pre_w[l:l + 1], bf(ffn2_w_gu[l]), bf(ffn2_w_down[l]), ffn2_post_w[l:l + 1])
    return h.reshape(b, s, d)
```
